```python
import math
import jax, jax.numpy as jnp
from jax import lax
import numpy as np

D_MODEL = 1024
BATCH = 16
SEQ = 2048
DEPTH = 4

MLA_HEADS = 6
MLA_Q_RANK = 256
MLA_KV_RANK = 128
MLA_NOPE = 64
MLA_ROPE = 32
MLA_V = 64
MLA_WIDTH = MLA_HEADS * MLA_V
ATTN_BLOCK = 128
RET_HEADS = 4
RET_DK = 32
RET_DV = 64
RET_WIDTH = RET_HEADS * RET_DV
RET_CHUNK = 128
LRU_WIDTH = D_MODEL - MLA_WIDTH - RET_WIDTH
LRU_BLOCKS = 6
LRU_BLOCK = LRU_WIDTH // LRU_BLOCKS
CONV_WIDTH = 4
LRU_C = 8.0
MIX_WIDTH = MLA_WIDTH + RET_WIDTH + LRU_WIDTH
FFN_HIDDEN = -(-8 * D_MODEL // (3 * 256)) * 256
IN_COLS = (MLA_Q_RANK, MLA_KV_RANK, MLA_ROPE,
           RET_HEADS * RET_DK, RET_HEADS * RET_DK, RET_WIDTH, RET_WIDTH,
           LRU_WIDTH, LRU_WIDTH)
D_IN = sum(IN_COLS)
ROPE_BASE = 10000.0
NORM_EPS = 1e-6
N_MOD = 6

kernel_name = "hymba_mla_retention_rglru_adaln"


def rms_norm(x, g):
    xf = x.astype(jnp.float32)
    y = xf * lax.rsqrt(jnp.mean(xf * xf, axis=-1, keepdims=True) + NORM_EPS)
    return (y * g.astype(jnp.float32)).astype(x.dtype)


def rotary(x, pos):
    half = x.shape[-1] // 2
    inv = ROPE_BASE ** (-jnp.arange(half, dtype=jnp.float32) / half)
    ang = pos.astype(jnp.float32)[:, None, :, None] * inv
    cos, sin = jnp.cos(ang), jnp.sin(ang)
    xf = x.astype(jnp.float32)
    x1, x2 = xf[..., :half], xf[..., half:]
    return jnp.concatenate([x1 * cos - x2 * sin, x1 * sin + x2 * cos], axis=-1).astype(x.dtype)


def split_cols(z):
    offsets = np.cumsum(np.array(IN_COLS))[:-1].tolist()
    return jnp.split(z, offsets, axis=-1)


def mla(c_q, c_kv, k_rope, pos, q_norm, w_uq, kv_norm, w_ukv):
    B, S, _ = c_q.shape
    H = MLA_HEADS
    q = (rms_norm(c_q, q_norm) @ w_uq).reshape(B, S, H, MLA_NOPE + MLA_ROPE).transpose(0, 2, 1, 3)
    q = jnp.concatenate([q[..., :MLA_NOPE], rotary(q[..., MLA_NOPE:], pos)], axis=-1)
    kv = (rms_norm(c_kv, kv_norm) @ w_ukv).reshape(B, S, H, MLA_NOPE + MLA_V).transpose(0, 2, 1, 3)
    k_nope, v = kv[..., :MLA_NOPE], kv[..., MLA_NOPE:]
    k_r = rotary(k_rope[:, None], pos)
    k = jnp.concatenate([k_nope, jnp.broadcast_to(k_r, (B, H, S, MLA_ROPE))], axis=-1)
    scale = (MLA_NOPE + MLA_ROPE) ** -0.5
    outs = []
    for blk in range(S // ATTN_BLOCK):
        q0 = blk * ATTN_BLOCK
        q1 = q0 + ATTN_BLOCK
        s = jnp.einsum('bhqd,bhkd->bhqk', q[:, :, q0:q1], k[:, :, :q1]).astype(jnp.float32) * scale
        mask = jnp.arange(q1)[None, :] <= jnp.arange(q0, q1)[:, None]
        s = jnp.where(mask, s, -jnp.inf)
        p = jax.nn.softmax(s, axis=-1).astype(v.dtype)
        outs.append(jnp.einsum('bhqk,bhkd->bhqd', p, v[:, :, :q1]))
    o = jnp.concatenate(outs, axis=2)
    return o.transpose(0, 2, 1, 3).reshape(B, S, H * MLA_V)


def retention(q, k, v, g, pos, gn_gain):
    B, S, _ = q.shape
    H, C = RET_HEADS, RET_CHUNK
    N = S // C
    f32 = jnp.float32
    q = rotary(q.reshape(B, S, H, RET_DK).transpose(0, 2, 1, 3), pos).astype(f32)
    k = rotary(k.reshape(B, S, H, RET_DK).transpose(0, 2, 1, 3), pos).astype(f32) * (RET_DK ** -0.5)
    v = v.reshape(B, S, H, RET_DV).transpose(0, 2, 1, 3).astype(f32)
    log_g = jnp.log(1.0 - jnp.exp2(-5.0 - jnp.arange(H, dtype=f32)))
    idx = jnp.arange(C, dtype=f32)
    diff = idx[:, None] - idx[None, :]
    decay = jnp.where(diff >= 0, jnp.exp(log_g[:, None, None] * jnp.maximum(diff, 0.0)), 0.0)
    q_decay = jnp.exp(log_g[:, None] * (idx + 1.0))
    k_decay = jnp.exp(log_g[:, None] * (C - 1.0 - idx))
    chunk_decay = jnp.exp(log_g * C)
    qc = q.reshape(B, H, N, C, RET_DK)
    kc = k.reshape(B, H, N, C, RET_DK)
    vc = v.reshape(B, H, N, C, RET_DV)
    scores = jnp.einsum('bhncd,bhnmd->bhncm', qc, kc) * decay[None, :, None]
    inner = jnp.einsum('bhncm,bhnme->bhnce', scores, vc)
    kv = jnp.einsum('bhnmd,bhnme->nbhde', kc * k_decay[None, :, None, :, None], vc)

    def step(state, kv_n):
        return state * chunk_decay[None, :, None, None] + kv_n, state

    _, states = lax.scan(step, jnp.zeros((B, H, RET_DK, RET_DV), f32), kv)
    cross = jnp.einsum('bhncd,nbhde->bhnce', qc * q_decay[None, :, None, :, None], states)
    o = (inner + cross).reshape(B, H, S, RET_DV)
    mu = jnp.mean(o, axis=-1, keepdims=True)
    var = jnp.mean(jnp.square(o - mu), axis=-1, keepdims=True)
    o = ((o - mu) * lax.rsqrt(var + NORM_EPS)).transpose(0, 2, 1, 3).reshape(B, S, H * RET_DV)
    o = o * gn_gain.astype(f32)
    return (jax.nn.silu(g.astype(f32)) * o).astype(g.dtype)


def rg_lru_block(xb, gb, conv_w, conv_b, w_a, b_a, w_i, b_i, lam):
    B, S, W = xb.shape
    f32 = jnp.float32
    xc = lax.conv_general_dilated(xb, conv_w[:, None, :], window_strides=(1,),
                                  padding=[(CONV_WIDTH - 1, 0)],
                                  dimension_numbers=('NWC', 'WIO', 'NWC'),
                                  feature_group_count=W) + conv_b
    xg = xc.reshape(B, S, LRU_BLOCKS, LRU_BLOCK)
    r = jax.nn.sigmoid(jnp.einsum('bsgi,gij->bsgj', xg, w_a).reshape(B, S, W) + b_a).astype(f32)
    i = jax.nn.sigmoid(jnp.einsum('bsgi,gij->bsgj', xg, w_i).reshape(B, S, W) + b_i).astype(f32)
    log_a = -LRU_C * r * jax.nn.softplus(-lam.astype(f32))
    a = jnp.exp(log_a)
    b = jnp.sqrt(-jnp.expm1(2.0 * log_a)) * i * xc.astype(f32)

    def combine(left, right):
        a1, b1 = left
        a2, b2 = right
        return a1 * a2, a2 * b1 + b2

    _, h = lax.associative_scan(combine, (a, b), axis=1)
    return (jax.nn.gelu(gb.astype(f32)) * h).astype(xb.dtype)


def setup_inputs(seed: int = 0) -> dict:
    key = jax.random.key(seed)
    ks = iter(jax.random.split(key, 40))
    L, D = DEPTH, D_MODEL

    def nrm(shape, scale):
        return jax.random.normal(next(ks), shape, jnp.float32) * scale

    x = nrm((BATCH, SEQ, D), 1.0)
    c = nrm((BATCH, D), 1.0)
    offset = jax.random.randint(next(ks), (BATCH, 1), 0, 1024, dtype=jnp.int32)
    positions = offset + jnp.arange(SEQ, dtype=jnp.int32)[None, :]
    a0 = jax.random.uniform(next(ks), (L, LRU_WIDTH), jnp.float32, 0.9, 0.999)
    return {
        "x": x,
        "c": c,
        "positions": positions,
        "mod_w": nrm((L, D, N_MOD * D), 0.5 * D ** -0.5),
        "mod_b": nrm((L, N_MOD * D), 0.02),
        "norm1": 1.0 + nrm((L, D), 0.02),
        "w_in": nrm((L, D, D_IN), D ** -0.5),
        "mla_q_norm": 1.0 + nrm((L, MLA_Q_RANK), 0.02),
        "mla_w_uq": nrm((L, MLA_Q_RANK, MLA_HEADS * (MLA_NOPE + MLA_ROPE)), MLA_Q_RANK ** -0.5),
        "mla_kv_norm": 1.0 + nrm((L, MLA_KV_RANK), 0.02),
        "mla_w_ukv": nrm((L, MLA_KV_RANK, MLA_HEADS * (MLA_NOPE + MLA_V)), MLA_KV_RANK ** -0.5),
        "ret_gn": 1.0 + nrm((L, RET_WIDTH), 0.02),
        "lru_conv_w": nrm((L, CONV_WIDTH, LRU_WIDTH), CONV_WIDTH ** -0.5),
        "lru_conv_b": nrm((L, LRU_WIDTH), 0.02),
        "lru_w_a": nrm((L, LRU_BLOCKS, LRU_BLOCK, LRU_BLOCK), LRU_BLOCK ** -0.5),
        "lru_b_a": nrm((L, LRU_WIDTH), 0.02),
        "lru_w_i": nrm((L, LRU_BLOCKS, LRU_BLOCK, LRU_BLOCK), LRU_BLOCK ** -0.5),
        "lru_b_i": nrm((L, LRU_WIDTH), 0.02),
        "lru_lambda": jnp.log(a0) - jnp.log1p(-a0),
        "w_out": nrm((L, MIX_WIDTH, D), MIX_WIDTH ** -0.5),
        "norm2": 1.0 + nrm((L, D), 0.02),
        "w_gate_up": nrm((L, D, 2 * FFN_HIDDEN), D ** -0.5),
        "w_down": nrm((L, FFN_HIDDEN, D), FFN_HIDDEN ** -0.5),
        "final_norm": 1.0 + nrm((D,), 0.02),
        "final_mod_w": nrm((D, 2 * D), 0.5 * D ** -0.5),
        "final_mod_b": nrm((2 * D,), 0.02),
    }


def reference(x, c, positions, mod_w, mod_b, norm1, w_in, mla_q_norm, mla_w_uq, mla_kv_norm,
              mla_w_ukv, ret_gn, lru_conv_w, lru_conv_b, lru_w_a, lru_b_a, lru_w_i, lru_b_i,
              lru_lambda, w_out, norm2, w_gate_up, w_down, final_norm, final_mod_w, final_mod_b):
    B = x.shape[0]
    c_act = jax.nn.silu(c)
    for l in range(DEPTH):
        mod = (c_act @ mod_w[l] + mod_b[l]).reshape(B, N_MOD, D_MODEL)[:, :, None, :]
        sh1, sc1, g1, sh2, sc2, g2 = [mod[:, j] for j in range(N_MOD)]
        h = rms_norm(x, norm1[l]) * (1.0 + sc1) + sh1
        c_q, c_kv, k_rope, r_q, r_k, r_v, r_g, u_x, u_g = split_cols(h @ w_in[l])
        y_a = mla(c_q, c_kv, k_rope, positions, mla_q_norm[l], mla_w_uq[l],
                  mla_kv_norm[l], mla_w_ukv[l])
        y_b = retention(r_q, r_k, r_v, r_g, positions, ret_gn[l])
        y_c = rg_lru_block(u_x, u_g, lru_conv_w[l], lru_conv_b[l], lru_w_a[l], lru_b_a[l],
                           lru_w_i[l], lru_b_i[l], lru_lambda[l])
        y = jnp.concatenate([y_a, y_b, y_c], axis=-1) @ w_out[l]
        x = x + g1 * y
        h = rms_norm(x, norm2[l]) * (1.0 + sc2) + sh2
        gate, up = jnp.split(h @ w_gate_up[l], 2, axis=-1)
        x = x + g2 * ((jax.nn.silu(gate) * up) @ w_down[l])
    f_shift, f_scale = jnp.split((c_act @ final_mod_w + final_mod_b)[:, None, :], 2, axis=-1)
    return rms_norm(x, final_norm) * (1.0 + f_scale) + f_shift
```

```python
import functools

import jax
import jax.numpy as jnp
from jax import lax
from jax.experimental import pallas as pl
from jax.experimental.pallas import tpu as pltpu

D_MODEL = 1024
MLA_HEADS = 6
MLA_Q_RANK = 256
MLA_KV_RANK = 128
MLA_NOPE = 64
MLA_ROPE = 32
MLA_V = 64
MLA_WIDTH = MLA_HEADS * MLA_V
RET_HEADS = 4
RET_DK = 32
RET_DV = 64
RET_WIDTH = RET_HEADS * RET_DV
RET_CHUNK = 128
LRU_WIDTH = D_MODEL - MLA_WIDTH - RET_WIDTH
LRU_BLOCKS = 6
LRU_BLOCK = LRU_WIDTH // LRU_BLOCKS
CONV_WIDTH = 4
LRU_C = 8.0
FFN_HIDDEN = 2816
ROPE_BASE = 10000.0
NORM_EPS = 1e-6
N_MOD = 6

LANES = 128
HEAD_PAD = LANES
MLA_PAD = MLA_HEADS * HEAD_PAD
Z_MLA = 512
Z_RET = 2 * RET_HEADS * RET_DK + 2 * RET_WIDTH
Z_LRU = 2 * LRU_WIDTH
Z_ALL = Z_MLA + Z_RET + Z_LRU
FFN_CHUNK = 256
VMEM_LIMIT = 56 * 1024 * 1024

BF16 = jnp.bfloat16
F32 = jnp.float32


def _params(*sem):
    return pltpu.CompilerParams(dimension_semantics=sem, vmem_limit_bytes=VMEM_LIMIT)


def _resident(shape, index_map):
    return pl.BlockSpec(shape, index_map, pipeline_mode=pl.Buffered(1))


def _rms(x, g):
    return x * lax.rsqrt(jnp.mean(x * x, axis=-1, keepdims=True) + NORM_EPS) * g


def _swap_halves(x, first_half):
    return jnp.where(first_half, pltpu.roll(x, LANES - 16, 1), pltpu.roll(x, 16, 1))


def _mod_kernel(c_ref, w_ref, b_ref, o_ref):
    c = c_ref[...]
    ca = (c * jax.nn.sigmoid(c)).astype(BF16)
    o_ref[...] = jnp.dot(ca, w_ref[...].astype(BF16), preferred_element_type=F32) + b_ref[...]


def _modulation(c, w, b, tn):
    L, D, N = w.shape
    B = c.shape[0]
    return pl.pallas_call(
        _mod_kernel,
        grid=(L, N // tn),
        in_specs=[pl.BlockSpec((B, D), lambda l, j: (0, 0)),
                  pl.BlockSpec((None, D, tn), lambda l, j: (l, 0, j)),
                  pl.BlockSpec((None, 1, tn), lambda l, j: (l, 0, j))],
        out_specs=pl.BlockSpec((None, B, tn), lambda l, j: (l, 0, j)),
        out_shape=jax.ShapeDtypeStruct((L, B, N), F32),
        compiler_params=_params("parallel", "parallel"),
        name="modulation",
    )(c, w, b.reshape(L, 1, N))


def _rope_table_kernel(pos_ref, inv_ref, cos_ref, sin_ref):
    ang = pos_ref[...].astype(F32) * inv_ref[...]
    lane = lax.broadcasted_iota(jnp.int32, ang.shape, 1)
    first_half = (lane % 32) < 16
    cos_ref[...] = jnp.cos(ang)
    sin = jnp.sin(ang)
    sin_ref[...] = jnp.where(first_half, -sin, sin)


def _rope_tables(positions):
    B, S = positions.shape
    half = MLA_ROPE // 2
    inv = ROPE_BASE ** (-jnp.arange(half, dtype=F32) / half)
    inv = jnp.tile(inv, LANES // half).reshape(1, LANES)
    ts = min(S, 1024)
    spec = pl.BlockSpec((None, ts, LANES), lambda b, i: (b, i, 0))
    return pl.pallas_call(
        _rope_table_kernel,
        grid=(B, S // ts),
        in_specs=[pl.BlockSpec((None, ts, 1), lambda b, i: (b, i, 0)),
                  pl.BlockSpec((1, LANES), lambda b, i: (0, 0))],
        out_specs=[spec, spec],
        out_shape=[jax.ShapeDtypeStruct((B, S, LANES), F32)] * 2,
        compiler_params=_params("parallel", "parallel"),
        name="rope_tables",
    )(positions.reshape(B, S, 1), inv)


def _in_proj_kernel(x_ref, mod_ref, g_ref, w_ref, zm_ref, zr_ref, zl_ref):
    mod = mod_ref[...]
    h = _rms(x_ref[...], g_ref[...]) * (1.0 + mod[1:2]) + mod[0:1]
    z = jnp.dot(h.astype(BF16), w_ref[...], preferred_element_type=F32)
    zm_ref[...] = z[:, :Z_MLA]
    zr_ref[...] = z[:, Z_MLA:Z_MLA + Z_RET]
    zl_ref[...] = z[:, Z_MLA + Z_RET:]


def _in_proj(x, mod, norm, w, l, tm):
    B, S, D = x.shape
    return pl.pallas_call(
        _in_proj_kernel,
        grid=(B, S // tm),
        in_specs=[pl.BlockSpec((None, tm, D), lambda b, i: (b, i, 0)),
                  pl.BlockSpec((None, None, N_MOD, D), lambda b, i: (l, b, 0, 0)),
                  pl.BlockSpec((None, 1, D), lambda b, i: (l, 0, 0)),
                  _resident((None, D, Z_ALL), lambda b, i: (l, 0, 0))],
        out_specs=[pl.BlockSpec((None, tm, Z_MLA), lambda b, i: (b, i, 0)),
                   pl.BlockSpec((None, tm, Z_RET), lambda b, i: (b, i, 0)),
                   pl.BlockSpec((tm, Z_LRU), lambda b, i: (i, b))],
        out_shape=[jax.ShapeDtypeStruct((B, S, Z_MLA), F32),
                   jax.ShapeDtypeStruct((B, S, Z_RET), F32),
                   jax.ShapeDtypeStruct((S, B * Z_LRU), F32)],
        compiler_params=_params("parallel", "parallel"),
        name="in_proj",
    )(x, mod, norm, w)


def _mla_prep_kernel(z_ref, cos_ref, sin_ref, qn_ref, kvn_ref, wq_ref, wk_ref, wr_ref, wv_ref,
                     q_ref, k_ref, v_ref):
    z = z_ref[...]
    cos = cos_ref[...]
    sin = sin_ref[...]
    lane = lax.broadcasted_iota(jnp.int32, cos.shape, 1)
    first_half = (lane % 32) < 16
    rope = (lane >= MLA_NOPE) & (lane < MLA_NOPE + MLA_ROPE)
    cq = jnp.where(rope, cos, 1.0)
    sq = jnp.where(rope, sin, 0.0)

    c_q = _rms(z[:, :MLA_Q_RANK], qn_ref[...]).astype(BF16)
    q = jnp.dot(c_q, wq_ref[...], preferred_element_type=F32)
    for h in range(MLA_HEADS):
        qh = q[:, h * HEAD_PAD:(h + 1) * HEAD_PAD]
        q_ref[:, h * HEAD_PAD:(h + 1) * HEAD_PAD] = (
            qh * cq + _swap_halves(qh, first_half) * sq).astype(BF16)

    c_kv = _rms(z[:, MLA_Q_RANK:MLA_Q_RANK + MLA_KV_RANK], kvn_ref[...]).astype(BF16)
    kr = z[:, MLA_Q_RANK + MLA_KV_RANK:]
    kr = jnp.where(lane < MLA_ROPE, kr * cos + _swap_halves(kr, first_half) * sin, 0.0)
    k = (jnp.dot(c_kv, wk_ref[...], preferred_element_type=F32)
         + jnp.dot(kr.astype(BF16), wr_ref[...], preferred_element_type=F32))
    k_ref[...] = k.astype(BF16)
    v_ref[...] = jnp.dot(c_kv, wv_ref[...], preferred_element_type=F32).astype(BF16)


def _mla_prep(zm, cos, sin, qn, kvn, wq, wk, wr, wv, l, tm):
    B, S, _ = zm.shape
    tok = lambda w: pl.BlockSpec((None, tm, w), lambda b, i: (b, i, 0))
    out = jax.ShapeDtypeStruct((B, S, MLA_PAD), BF16)
    return pl.pallas_call(
        _mla_prep_kernel,
        grid=(B, S // tm),
        in_specs=[tok(Z_MLA), tok(LANES), tok(LANES),
                  pl.BlockSpec((None, 1, MLA_Q_RANK), lambda b, i: (l, 0, 0)),
                  pl.BlockSpec((None, 1, MLA_KV_RANK), lambda b, i: (l, 0, 0)),
                  _resident((None, MLA_Q_RANK, MLA_PAD), lambda b, i: (l, 0, 0)),
                  _resident((None, MLA_KV_RANK, MLA_PAD), lambda b, i: (l, 0, 0)),
                  _resident((LANES, MLA_PAD), lambda b, i: (0, 0)),
                  _resident((None, MLA_KV_RANK, MLA_PAD), lambda b, i: (l, 0, 0))],
        out_specs=[tok(MLA_PAD)] * 3,
        out_shape=[out] * 3,
        compiler_params=_params("parallel", "parallel"),
        name="mla_prep",
    )(zm, cos, sin, qn, kvn, wq, wk, wr, wv)


def _attn_kernel(q_ref, k_ref, v_ref, o_ref, *, tq):
    i = pl.program_id(1)
    scale = (MLA_NOPE + MLA_ROPE) ** -0.5
    row = lax.broadcasted_iota(jnp.int32, (tq, tq), 0)
    col = lax.broadcasted_iota(jnp.int32, (tq, tq), 1)
    causal = col <= row

    def head_out(h):
        cols = slice(h * HEAD_PAD, (h + 1) * HEAD_PAD)
        qh = q_ref[:, cols]

        def step(j, carry, masked):
            m, l, acc = carry
            rows = pl.ds(pl.multiple_of(j * tq, tq), tq)
            s = lax.dot_general(qh, k_ref[rows, cols], (((1,), (1,)), ((), ())),
                                preferred_element_type=F32) * scale
            if masked:
                s = jnp.where(causal, s, -jnp.inf)
            m_new = jnp.maximum(m, jnp.max(s, axis=-1, keepdims=True))
            alpha = jnp.exp(m - m_new)
            p = jnp.exp(s - m_new)
            l = alpha * l + jnp.sum(p, axis=-1, keepdims=True)
            acc = alpha * acc + jnp.dot(p.astype(BF16), v_ref[rows, cols],
                                        preferred_element_type=F32)
            return m_new, l, acc

        init = (jnp.full((tq, 1), -jnp.inf, F32), jnp.zeros((tq, 1), F32),
                jnp.zeros((tq, HEAD_PAD), F32))
        carry = lax.fori_loop(0, i, functools.partial(step, masked=False), init)
        _, l, acc = step(i, carry, True)
        return acc / l

    for p in range(MLA_HEADS // 2):
        o_ref[:, p * LANES:(p + 1) * LANES] = head_out(2 * p) + head_out(2 * p + 1)


def _attention(q, k, v, tq):
    B, S, _ = q.shape
    return pl.pallas_call(
        functools.partial(_attn_kernel, tq=tq),
        grid=(B, S // tq),
        in_specs=[pl.BlockSpec((None, tq, MLA_PAD), lambda b, i: (b, i, 0)),
                  pl.BlockSpec((None, S, MLA_PAD), lambda b, i: (b, 0, 0)),
                  pl.BlockSpec((None, S, MLA_PAD), lambda b, i: (b, 0, 0))],
        out_specs=pl.BlockSpec((None, tq, MLA_WIDTH), lambda b, i: (b, i, 0)),
        out_shape=jax.ShapeDtypeStruct((B, S, MLA_WIDTH), F32),
        compiler_params=_params("parallel", "arbitrary"),
        name="mla_attention",
    )(q, k, v)


def _retention_consts():
    C, H = RET_CHUNK, RET_HEADS
    log_g = jnp.log(1.0 - jnp.exp2(-5.0 - jnp.arange(H, dtype=F32)))
    idx = jnp.arange(C, dtype=F32)
    diff = idx[:, None] - idx[None, :]
    decay = jnp.where(diff >= 0, jnp.exp(log_g[:, None, None] * jnp.maximum(diff, 0.0)), 0.0)
    q_decay = jnp.exp(log_g[:, None] * (idx + 1.0))
    k_decay = jnp.exp(log_g[:, None] * (C - 1.0 - idx))
    chunk_decay = jnp.exp(log_g * C)
    qd = jnp.repeat(q_decay.T, RET_DK, axis=1)
    kd = jnp.repeat(k_decay.T, RET_DK, axis=1)
    cd = jnp.broadcast_to(jnp.repeat(chunk_decay, RET_DK)[:, None], (H * RET_DK, RET_WIDTH))
    return decay, qd, kd, cd


def _retention_kernel(z_ref, cos_ref, sin_ref, dec_ref, qd_ref, kd_ref, cd_ref, gn_ref, o_ref,
                      *, n_chunks):
    C = RET_CHUNK
    lane = lax.broadcasted_iota(jnp.int32, (C, LANES), 1)
    first_half = (lane % 32) < 16
    low = lane < RET_DV
    head_of_lane = lane // RET_DK
    srow = lax.broadcasted_iota(jnp.int32, (LANES, RET_WIDTH), 0) // RET_DK
    scol = lax.broadcasted_iota(jnp.int32, (LANES, RET_WIDTH), 1) // RET_DV
    same_head = srow == scol
    gn = gn_ref[...]
    cd = cd_ref[...]

    def chunk(n, state):
        rows = pl.ds(pl.multiple_of(n * C, C), C)
        cos = cos_ref[rows, :]
        sin = sin_ref[rows, :]
        q = z_ref[rows, 0:LANES]
        k = z_ref[rows, LANES:2 * LANES]
        v = z_ref[rows, 2 * LANES:2 * LANES + RET_WIDTH].astype(BF16)
        g = z_ref[rows, 2 * LANES + RET_WIDTH:]
        q = q * cos + _swap_halves(q, first_half) * sin
        k = (k * cos + _swap_halves(k, first_half) * sin) * (RET_DK ** -0.5)
        kb = k.astype(BF16)

        halves = []
        for p in range(RET_HEADS // 2):
            vp = v[:, p * LANES:(p + 1) * LANES]
            r = []
            for h in (2 * p, 2 * p + 1):
                qm = jnp.where(head_of_lane == h, q, 0.0).astype(BF16)
                s = lax.dot_general(qm, kb, (((1,), (1,)), ((), ())),
                                    preferred_element_type=F32) * dec_ref[h]
                r.append(jnp.dot(s.astype(BF16), vp, preferred_element_type=F32))
            halves.append(jnp.where(low, r[0], r[1]))
        cross = jnp.dot((q * qd_ref[...]).astype(BF16), state.astype(BF16),
                        preferred_element_type=F32)

        kv = lax.dot_general((k * kd_ref[...]).astype(BF16), v, (((0,), (0,)), ((), ())),
                             preferred_element_type=F32)
        new_state = state * cd + jnp.where(same_head, kv, 0.0)

        for p in range(RET_HEADS // 2):
            cols = slice(p * LANES, (p + 1) * LANES)
            o = halves[p] + cross[:, cols]
            s_lo = jnp.sum(jnp.where(low, o, 0.0), axis=-1, keepdims=True)
            s_hi = jnp.sum(jnp.where(low, 0.0, o), axis=-1, keepdims=True)
            d = o - jnp.where(low, s_lo, s_hi) * (1.0 / RET_DV)
            d2 = d * d
            v_lo = jnp.sum(jnp.where(low, d2, 0.0), axis=-1, keepdims=True)
            v_hi = jnp.sum(jnp.where(low, 0.0, d2), axis=-1, keepdims=True)
            var = jnp.where(low, v_lo, v_hi) * (1.0 / RET_DV)
            y = d * lax.rsqrt(var + NORM_EPS) * gn[:, cols]
            gp = g[:, cols]
            o_ref[rows, cols] = gp * jax.nn.sigmoid(gp) * y
        return new_state

    lax.fori_loop(0, n_chunks, chunk, jnp.zeros((LANES, RET_WIDTH), F32))


def _retention(zr, cos, sin, consts, gn, l):
    B, S, _ = zr.shape
    decay, qd, kd, cd = consts
    full = lambda a: pl.BlockSpec(a.shape, lambda b: (0,) * a.ndim)
    return pl.pallas_call(
        functools.partial(_retention_kernel, n_chunks=S // RET_CHUNK),
        grid=(B,),
        in_specs=[pl.BlockSpec((None, S, Z_RET), lambda b: (b, 0, 0)),
                  pl.BlockSpec((None, S, LANES), lambda b: (b, 0, 0)),
                  pl.BlockSpec((None, S, LANES), lambda b: (b, 0, 0)),
                  full(decay), full(qd), full(kd), full(cd),
                  pl.BlockSpec((None, 1, RET_WIDTH), lambda b: (l, 0, 0))],
        out_specs=pl.BlockSpec((None, S, RET_WIDTH), lambda b: (b, 0, 0)),
        out_shape=jax.ShapeDtypeStruct((B, S, RET_WIDTH), F32),
        compiler_params=_params("parallel"),
        name="retention",
    )(zr, cos, sin, decay, qd, kd, cd, gn)


def _softplus(x):
    return jnp.maximum(x, 0.0) + jnp.log1p(jnp.exp(-jnp.abs(x)))


def _gelu_tanh(x):
    return 0.5 * x * (1.0 + jnp.tanh(0.7978845608028654 * (x + 0.044715 * (x * x * x))))


def _lru_kernel(z_ref, cw_ref, cb_ref, wa_ref, ba_ref, wi_ref, bi_ref, lam_ref, o_ref,
                xbuf, a_s, b_s, h_s, *, ts):
    W = LRU_WIDTH
    B = z_ref.shape[1]
    PAD = 8

    @pl.when(pl.program_id(0) == 0)
    def _():
        xbuf[0:PAD] = jnp.zeros((PAD, B, W), F32)
        h_s[...] = jnp.zeros((B, W), F32)

    xbuf[PAD:PAD + ts] = z_ref[:, :, 0:W]
    cw = cw_ref[...]
    xc = cb_ref[...].reshape(1, 1, W) + sum(
        xbuf[PAD - (CONV_WIDTH - 1) + j:PAD - (CONV_WIDTH - 1) + j + ts] * cw[j].reshape(1, 1, W)
        for j in range(CONV_WIDTH))
    xbuf[PAD - (CONV_WIDTH - 1):PAD] = xbuf[PAD + ts - (CONV_WIDTH - 1):PAD + ts]

    xc2 = xc.reshape(ts * B, W)
    xb = xc2.astype(BF16)
    r = jax.nn.sigmoid(jnp.dot(xb, wa_ref[...], preferred_element_type=F32) + ba_ref[...])
    i = jax.nn.sigmoid(jnp.dot(xb, wi_ref[...], preferred_element_type=F32) + bi_ref[...])
    log_a = (-LRU_C * _softplus(-lam_ref[...])) * r
    a = jnp.exp(log_a)
    a_s[...] = a.reshape(ts, B, W)
    one_minus_a2 = -jnp.tanh(log_a) * (a * a + 1.0)
    b_s[...] = (jnp.sqrt(one_minus_a2) * i * xc2).reshape(ts, B, W)

    def step(t, h):
        h = a_s[t] * h + b_s[t]
        b_s[t] = h
        return h

    h_s[...] = lax.fori_loop(0, ts, step, h_s[...], unroll=8)
    o_ref[...] = _gelu_tanh(z_ref[:, :, W:]) * b_s[...]


def _rg_lru(zl, cw, cb, wa, ba, wi, bi, lam, l, ts):
    S, B, _ = zl.shape
    W = LRU_WIDTH
    vec = pl.BlockSpec((None, 1, W), lambda i: (l, 0, 0))
    mat = pl.BlockSpec((None, W, W), lambda i: (l, 0, 0))
    return pl.pallas_call(
        functools.partial(_lru_kernel, ts=ts),
        grid=(S // ts,),
        in_specs=[pl.BlockSpec((ts, B, Z_LRU), lambda i: (i, 0, 0)),
                  pl.BlockSpec((None, CONV_WIDTH, W), lambda i: (l, 0, 0)),
                  vec, mat, vec, mat, vec, vec],
        out_specs=pl.BlockSpec((ts, B, W), lambda i: (i, 0, 0)),
        out_shape=jax.ShapeDtypeStruct((S, B, W), F32),
        scratch_shapes=[pltpu.VMEM((ts + 8, B, W), F32), pltpu.VMEM((ts, B, W), F32),
                        pltpu.VMEM((ts, B, W), F32), pltpu.VMEM((B, W), F32)],
        compiler_params=_params("arbitrary"),
        name="rg_lru",
    )(zl, cw, cb, wa, ba, wi, bi, lam)


def _out_ffn_kernel(x_ref, ya_ref, yb_ref, yc_ref, mod_ref, g_ref, wo_ref, wg_ref, wu_ref,
                    wd_ref, o_ref, h_s, acc_s):
    mod = mod_ref[...]
    a0, b0 = MLA_WIDTH, MLA_WIDTH + RET_WIDTH
    y = (jnp.dot(ya_ref[...].astype(BF16), wo_ref[0:a0, :], preferred_element_type=F32)
         + jnp.dot(yb_ref[...].astype(BF16), wo_ref[a0:b0, :], preferred_element_type=F32)
         + jnp.dot(yc_ref[...].astype(BF16), wo_ref[b0:, :], preferred_element_type=F32))
    x1 = x_ref[...] + mod[2:3] * y
    o_ref[...] = x1
    h_s[...] = (_rms(x1, g_ref[...]) * (1.0 + mod[4:5]) + mod[3:4]).astype(BF16)
    acc_s[...] = jnp.zeros_like(acc_s)

    def chunk(j, _):
        cols = pl.ds(pl.multiple_of(j * FFN_CHUNK, FFN_CHUNK), FFN_CHUNK)
        h = h_s[...]
        gate = jnp.dot(h, wg_ref[:, cols], preferred_element_type=F32)
        up = jnp.dot(h, wu_ref[:, cols], preferred_element_type=F32)
        act = (gate * jax.nn.sigmoid(gate) * up).astype(BF16)
        acc_s[...] += jnp.dot(act, wd_ref[cols, :], preferred_element_type=F32)
        return 0

    lax.fori_loop(0, FFN_HIDDEN // FFN_CHUNK, chunk, 0)
    o_ref[...] += mod[5:6] * acc_s[...]


def _out_ffn(x, ya, yb, yc, mod, norm, wo, wgu, wd, l, tm):
    B, S, D = x.shape
    tok = lambda w: pl.BlockSpec((None, tm, w), lambda b, i: (b, i, 0))
    return pl.pallas_call(
        _out_ffn_kernel,
        grid=(B, S // tm),
        in_specs=[tok(D), tok(MLA_WIDTH), tok(RET_WIDTH),
                  pl.BlockSpec((tm, LRU_WIDTH), lambda b, i: (i, b)),
                  pl.BlockSpec((None, None, N_MOD, D), lambda b, i: (l, b, 0, 0)),
                  pl.BlockSpec((None, 1, D), lambda b, i: (l, 0, 0)),
                  _resident((None, D, D), lambda b, i: (l, 0, 0)),
                  _resident((None, D, FFN_HIDDEN), lambda b, i: (l, 0, 0)),
                  _resident((None, D, FFN_HIDDEN), lambda b, i: (l, 0, 1)),
                  _resident((None, FFN_HIDDEN, D), lambda b, i: (l, 0, 0))],
        out_specs=tok(D),
        out_shape=jax.ShapeDtypeStruct((B, S, D), F32),
        scratch_shapes=[pltpu.VMEM((tm, D), BF16), pltpu.VMEM((tm, D), F32)],
        compiler_params=_params("parallel", "parallel"),
        name="out_ffn",
    )(x, ya, yb, yc, mod, norm, wo, wgu, wgu, wd)


def _final_kernel(x_ref, mod_ref, g_ref, o_ref):
    mod = mod_ref[...]
    o_ref[...] = _rms(x_ref[...], g_ref[...]) * (1.0 + mod[1:2]) + mod[0:1]


def _final_norm(x, mod, norm, tm):
    B, S, D = x.shape
    return pl.pallas_call(
        _final_kernel,
        grid=(B, S // tm),
        in_specs=[pl.BlockSpec((None, tm, D), lambda b, i: (b, i, 0)),
                  pl.BlockSpec((None, 2, D), lambda b, i: (b, 0, 0)),
                  pl.BlockSpec((1, D), lambda b, i: (0, 0))],
        out_specs=pl.BlockSpec((None, tm, D), lambda b, i: (b, i, 0)),
        out_shape=jax.ShapeDtypeStruct((B, S, D), F32),
        compiler_params=_params("parallel", "parallel"),
        name="final_norm",
    )(x, mod, norm)


def _pad_in_proj(w_in):
    n_mla = MLA_Q_RANK + MLA_KV_RANK + MLA_ROPE
    pad = jnp.zeros(w_in.shape[:2] + (Z_MLA - n_mla,), w_in.dtype)
    return jnp.concatenate([w_in[..., :n_mla], pad, w_in[..., n_mla:]], axis=-1).astype(BF16)


def _pad_heads(w, width):
    L, R, H, _ = w.shape
    return jnp.pad(w, ((0, 0), (0, 0), (0, 0), (0, HEAD_PAD - width))).reshape(L, R, H * HEAD_PAD)


def _mla_weights(w_uq, w_ukv):
    L = w_uq.shape[0]
    wq = _pad_heads(w_uq.reshape(L, MLA_Q_RANK, MLA_HEADS, MLA_NOPE + MLA_ROPE), MLA_NOPE + MLA_ROPE)
    kv = w_ukv.reshape(L, MLA_KV_RANK, MLA_HEADS, MLA_NOPE + MLA_V)
    wk = _pad_heads(kv[..., :MLA_NOPE], MLA_NOPE)
    v = kv[..., MLA_NOPE:]
    odd = (jnp.arange(MLA_HEADS) % 2 == 1)[None, None, :, None]
    zero = jnp.zeros_like(v)
    wv = jnp.concatenate([jnp.where(odd, zero, v), jnp.where(odd, v, zero)], axis=-1)
    wv = wv.reshape(L, MLA_KV_RANK, MLA_PAD)
    r = jnp.arange(LANES)[:, None]
    c = jnp.arange(MLA_PAD)[None, :]
    wr = ((r < MLA_ROPE) & (c % HEAD_PAD == MLA_NOPE + r)).astype(BF16)
    return wq.astype(BF16), wk.astype(BF16), wr, wv.astype(BF16)


def _block_diag(w):
    L, G, I, J = w.shape
    eye = jnp.eye(G, dtype=w.dtype)
    return jnp.einsum('lgij,gh->lgihj', w, eye).reshape(L, G * I, G * J)


def kernel(x, c, positions, mod_w, mod_b, norm1, w_in, mla_q_norm, mla_w_uq, mla_kv_norm, mla_w_ukv, ret_gn, lru_conv_w, lru_conv_b, lru_w_a, lru_b_a, lru_w_i, lru_b_i, lru_lambda, w_out, norm2, w_gate_up, w_down, final_norm, final_mod_w, final_mod_b):
    B, S, D = x.shape
    L = mod_w.shape[0]
    tm = min(S, 512)
    tq = min(S, 256)
    ts = min(S, 128)
    row = lambda a: a.reshape(L, 1, a.shape[-1])

    mod = _modulation(c, mod_w, mod_b, 1536).reshape(L, B, N_MOD, D)
    fmod = _modulation(c, final_mod_w[None], final_mod_b[None], 1024).reshape(B, 2, D)
    cos, sin = _rope_tables(positions)

    w_in_p = _pad_in_proj(w_in)
    wq, wk, wr, wv = _mla_weights(mla_w_uq, mla_w_ukv)
    wa = _block_diag(lru_w_a).astype(BF16)
    wi = _block_diag(lru_w_i).astype(BF16)
    wo = w_out.astype(BF16)
    wgu = w_gate_up.astype(BF16)
    wd = w_down.astype(BF16)
    ret_consts = _retention_consts()

    for l in range(L):
        zm, zr, zl = _in_proj(x, mod, row(norm1), w_in_p, l, tm)
        q, k, v = _mla_prep(zm, cos, sin, row(mla_q_norm), row(mla_kv_norm), wq, wk, wr, wv, l, tm)
        ya = _attention(q, k, v, tq)
        yb = _retention(zr, cos, sin, ret_consts, row(ret_gn), l)
        yc = _rg_lru(zl.reshape(S, B, Z_LRU), lru_conv_w, row(lru_conv_b), wa, row(lru_b_a),
                     wi, row(lru_b_i), row(lru_lambda), l, ts)
        x = _out_ffn(x, ya, yb, yc.reshape(S, B * LRU_WIDTH), mod, row(norm2), wo, wgu, wd, l, tm)
    return _final_norm(x, fmod, final_norm.reshape(1, D), tm)
```

```python
import functools

import jax
import jax.numpy as jnp
from jax import lax
from jax.experimental import pallas as pl
from jax.experimental.pallas import tpu as pltpu

D_MODEL = 1024
MLA_HEADS = 6
MLA_Q_RANK = 256
MLA_KV_RANK = 128
MLA_NOPE = 64
MLA_ROPE = 32
MLA_V = 64
MLA_WIDTH = MLA_HEADS * MLA_V
RET_HEADS = 4
RET_DK = 32
RET_DV = 64
RET_WIDTH = RET_HEADS * RET_DV
RET_CHUNK = 128
RET_GROUP = 4
LRU_WIDTH = D_MODEL - MLA_WIDTH - RET_WIDTH
LRU_BLOCKS = 6
LRU_BLOCK = LRU_WIDTH // LRU_BLOCKS
CONV_WIDTH = 4
LRU_C = 8.0
FFN_HIDDEN = 2816
ROPE_BASE = 10000.0
NORM_EPS = 1e-6
LOG2_E = 1.4426950408889634
N_MOD = 6

LANES = 128
HEAD_PAD = LANES
MLA_PAD = MLA_HEADS * HEAD_PAD
Z_MLA = 512
Z_RET = 2 * RET_HEADS * RET_DK + 2 * RET_WIDTH
Z_LRU = 2 * LRU_WIDTH
Z_ALL = Z_MLA + Z_RET + Z_LRU
FFN_CHUNK = 256
VMEM_LIMIT = 56 * 1024 * 1024

BF16 = jnp.bfloat16
F32 = jnp.float32


def _params(*sem):
    return pltpu.CompilerParams(dimension_semantics=sem, vmem_limit_bytes=VMEM_LIMIT)


def _resident(shape, index_map):
    return pl.BlockSpec(shape, index_map, pipeline_mode=pl.Buffered(1))


def _rms(x, g):
    return x * lax.rsqrt(jnp.mean(x * x, axis=-1, keepdims=True) + NORM_EPS) * g


def _swap_halves(x, first_half):
    return jnp.where(first_half, pltpu.roll(x, LANES - 16, 1), pltpu.roll(x, 16, 1))


def _mod_kernel(c_ref, w_ref, b_ref, o_ref):
    c = c_ref[...]
    ca = (c * jax.nn.sigmoid(c)).astype(BF16)
    o_ref[...] = jnp.dot(ca, w_ref[...].astype(BF16), preferred_element_type=F32) + b_ref[...]


def _modulation(c, w, b, tn):
    L, D, N = w.shape
    B = c.shape[0]
    return pl.pallas_call(
        _mod_kernel,
        grid=(L, N // tn),
        in_specs=[pl.BlockSpec((B, D), lambda l, j: (0, 0)),
                  pl.BlockSpec((None, D, tn), lambda l, j: (l, 0, j)),
                  pl.BlockSpec((None, 1, tn), lambda l, j: (l, 0, j))],
        out_specs=pl.BlockSpec((None, B, tn), lambda l, j: (l, 0, j)),
        out_shape=jax.ShapeDtypeStruct((L, B, N), F32),
        compiler_params=_params("parallel", "parallel"),
        name="modulation",
    )(c, w, b.reshape(L, 1, N))


def _rope_table_kernel(pos_ref, inv_ref, cos_ref, sin_ref):
    ang = pos_ref[...].astype(F32) * inv_ref[...]
    lane = lax.broadcasted_iota(jnp.int32, ang.shape, 1)
    first_half = (lane % 32) < 16
    cos_ref[...] = jnp.cos(ang)
    sin = jnp.sin(ang)
    sin_ref[...] = jnp.where(first_half, -sin, sin)


def _rope_tables(positions):
    B, S = positions.shape
    half = MLA_ROPE // 2
    inv = ROPE_BASE ** (-jnp.arange(half, dtype=F32) / half)
    inv = jnp.tile(inv, LANES // half).reshape(1, LANES)
    ts = min(S, 1024)
    spec = pl.BlockSpec((None, ts, LANES), lambda b, i: (b, i, 0))
    return pl.pallas_call(
        _rope_table_kernel,
        grid=(B, S // ts),
        in_specs=[pl.BlockSpec((None, ts, 1), lambda b, i: (b, i, 0)),
                  pl.BlockSpec((1, LANES), lambda b, i: (0, 0))],
        out_specs=[spec, spec],
        out_shape=[jax.ShapeDtypeStruct((B, S, LANES), F32)] * 2,
        compiler_params=_params("parallel", "parallel"),
        name="rope_tables",
    )(positions.reshape(B, S, 1), inv)


def _in_proj_kernel(x_ref, mod_ref, g_ref, w_ref, zm_ref, zr_ref, zl_ref):
    mod = mod_ref[...]
    h = _rms(x_ref[...], g_ref[...]) * (1.0 + mod[1:2]) + mod[0:1]
    z = jnp.dot(h.astype(BF16), w_ref[...], preferred_element_type=F32)
    zm_ref[...] = z[:, :Z_MLA]
    zr_ref[...] = z[:, Z_MLA:Z_MLA + Z_RET]
    zl_ref[...] = z[:, Z_MLA + Z_RET:]


def _in_proj(x, mod, norm, w, l, tm):
    B, S, D = x.shape
    return pl.pallas_call(
        _in_proj_kernel,
        grid=(B, S // tm),
        in_specs=[pl.BlockSpec((None, tm, D), lambda b, i: (b, i, 0)),
                  pl.BlockSpec((None, None, N_MOD, D), lambda b, i: (l, b, 0, 0)),
                  pl.BlockSpec((None, 1, D), lambda b, i: (l, 0, 0)),
                  _resident((None, D, Z_ALL), lambda b, i: (l, 0, 0))],
        out_specs=[pl.BlockSpec((None, tm, Z_MLA), lambda b, i: (b, i, 0)),
                   pl.BlockSpec((None, tm, Z_RET), lambda b, i: (b, i, 0)),
                   pl.BlockSpec((tm, Z_LRU), lambda b, i: (i, b))],
        out_shape=[jax.ShapeDtypeStruct((B, S, Z_MLA), F32),
                   jax.ShapeDtypeStruct((B, S, Z_RET), F32),
                   jax.ShapeDtypeStruct((S, B * Z_LRU), F32)],
        compiler_params=_params("parallel", "parallel"),
        name="in_proj",
    )(x, mod, norm, w)


def _mla_prep_kernel(z_ref, cos_ref, sin_ref, qn_ref, kvn_ref, wq_ref, wk_ref, wr_ref, wv_ref,
                     one_ref, qt_ref, k_ref, vt_ref, *, tq):
    z = z_ref[...]
    cos = cos_ref[...]
    sin = sin_ref[...]
    lane = lax.broadcasted_iota(jnp.int32, cos.shape, 1)
    first_half = (lane % 32) < 16
    rope = (lane >= MLA_NOPE) & (lane < MLA_NOPE + MLA_ROPE)
    qk_scale = (MLA_NOPE + MLA_ROPE) ** -0.5 * LOG2_E
    cq = jnp.where(rope, cos, 1.0) * qk_scale
    sq = jnp.where(rope, sin, 0.0) * qk_scale

    c_q = _rms(z[:, :MLA_Q_RANK], qn_ref[...]).astype(BF16)
    q = jnp.dot(c_q, wq_ref[...], preferred_element_type=F32)
    q = jnp.concatenate(
        [q[:, h * HEAD_PAD:(h + 1) * HEAD_PAD] * cq
         + _swap_halves(q[:, h * HEAD_PAD:(h + 1) * HEAD_PAD], first_half) * sq
         for h in range(MLA_HEADS)], axis=1)

    c_kv = _rms(z[:, MLA_Q_RANK:MLA_Q_RANK + MLA_KV_RANK], kvn_ref[...]).astype(BF16)
    kr = z[:, MLA_Q_RANK + MLA_KV_RANK:]
    kr = jnp.where(lane < MLA_ROPE, kr * cos + _swap_halves(kr, first_half) * sin, 0.0)
    k = (jnp.dot(c_kv, wk_ref[...], preferred_element_type=F32)
         + jnp.dot(kr.astype(BF16), wr_ref[...], preferred_element_type=F32))
    k_ref[...] = k.astype(BF16)
    v = jnp.dot(c_kv, wv_ref[...], preferred_element_type=F32) + one_ref[...]
    for t in range(z.shape[0] // tq):
        rows = slice(t * tq, (t + 1) * tq)
        qt_ref[t] = q[rows].T.astype(BF16)
        vt_ref[t] = v[rows].T.astype(BF16)


def _mla_prep(zm, cos, sin, qn, kvn, wq, wk, wr, wv, ones, l, tm, tq):
    B, S, _ = zm.shape
    tok = lambda w: pl.BlockSpec((None, tm, w), lambda b, i: (b, i, 0))
    tr_spec = pl.BlockSpec((None, tm // tq, MLA_PAD, tq), lambda b, i: (b, i, 0, 0))
    tr_shape = jax.ShapeDtypeStruct((B, S // tq, MLA_PAD, tq), BF16)
    return pl.pallas_call(
        functools.partial(_mla_prep_kernel, tq=tq),
        grid=(B, S // tm),
        in_specs=[tok(Z_MLA), tok(LANES), tok(LANES),
                  pl.BlockSpec((None, 1, MLA_Q_RANK), lambda b, i: (l, 0, 0)),
                  pl.BlockSpec((None, 1, MLA_KV_RANK), lambda b, i: (l, 0, 0)),
                  _resident((None, MLA_Q_RANK, MLA_PAD), lambda b, i: (l, 0, 0)),
                  _resident((None, MLA_KV_RANK, MLA_PAD), lambda b, i: (l, 0, 0)),
                  _resident((LANES, MLA_PAD), lambda b, i: (0, 0)),
                  _resident((None, MLA_KV_RANK, MLA_PAD), lambda b, i: (l, 0, 0)),
                  pl.BlockSpec((1, MLA_PAD), lambda b, i: (0, 0))],
        out_specs=[tr_spec, tok(MLA_PAD), tr_spec],
        out_shape=[tr_shape, jax.ShapeDtypeStruct((B, S, MLA_PAD), BF16), tr_shape],
        compiler_params=_params("parallel", "parallel"),
        name="mla_prep",
    )(zm, cos, sin, qn, kvn, wq, wk, wr, wv, ones)


def _attn_kernel(qt_ref, k_ref, vt_ref, o_ref, s_s, m_s, acc_s, *, tq):
    i = pl.program_id(1)
    groups = tq // 8
    key = lax.broadcasted_iota(jnp.int32, (tq, tq), 0)
    qry = lax.broadcasted_iota(jnp.int32, (tq, tq), 1)
    causal = key <= qry
    hs = lambda h: slice(h * HEAD_PAD, (h + 1) * HEAD_PAD)

    def scores(j, masked):
        rows = pl.ds(pl.multiple_of(j * tq, tq), tq)
        for h in range(MLA_HEADS):
            s = jnp.dot(k_ref[rows, hs(h)], qt_ref[hs(h), :], preferred_element_type=F32)
            if masked:
                s = jnp.where(causal, s, -jnp.inf)
            s_s[h, j] = s
            m_s[h] = jnp.maximum(m_s[h], jnp.max(s.reshape(groups, 8, tq), axis=0))

    def values(j):
        for h in range(MLA_HEADS):
            p = jnp.exp2(s_s[h, j].reshape(groups, 8, tq) - m_s[h][None])
            acc_s[h] += jnp.dot(vt_ref[j, hs(h), :], p.reshape(tq, tq).astype(BF16),
                                preferred_element_type=F32)

    def scores_body(j, carry):
        scores(j, False)
        return carry

    def values_body(j, carry):
        values(j)
        return carry

    m_s[...] = jnp.full(m_s.shape, -jnp.inf, F32)
    lax.fori_loop(0, i, scores_body, 0)
    scores(i, True)
    for h in range(MLA_HEADS):
        m_s[h] = jnp.broadcast_to(jnp.max(m_s[h], axis=0, keepdims=True), (8, tq))
    acc_s[...] = jnp.zeros(acc_s.shape, F32)
    lax.fori_loop(0, i + 1, values_body, 0)

    low = lax.broadcasted_iota(jnp.int32, (HEAD_PAD, tq), 0) < MLA_V
    for p in range(MLA_HEADS // 2):
        even = acc_s[2 * p]
        odd = acc_s[2 * p + 1]
        pair = jnp.where(low, even / even[MLA_V:MLA_V + 1], odd / odd[0:1])
        o_ref[:, p * LANES:(p + 1) * LANES] = pair.T.astype(BF16)


def _attention(qt, k, vt, tq):
    B, S, _ = k.shape
    nq = S // tq
    return pl.pallas_call(
        functools.partial(_attn_kernel, tq=tq),
        grid=(B, nq),
        in_specs=[pl.BlockSpec((None, None, MLA_PAD, tq), lambda b, i: (b, i, 0, 0)),
                  pl.BlockSpec((None, S, MLA_PAD), lambda b, i: (b, 0, 0)),
                  pl.BlockSpec((None, nq, MLA_PAD, tq), lambda b, i: (b, 0, 0, 0))],
        out_specs=pl.BlockSpec((None, tq, MLA_WIDTH), lambda b, i: (b, i, 0)),
        out_shape=jax.ShapeDtypeStruct((B, S, MLA_WIDTH), BF16),
        scratch_shapes=[pltpu.VMEM((MLA_HEADS, nq, tq, tq), F32),
                        pltpu.VMEM((MLA_HEADS, 8, tq), F32),
                        pltpu.VMEM((MLA_HEADS, HEAD_PAD, tq), F32)],
        compiler_params=_params("parallel", "arbitrary"),
        name="mla_attention",
    )(qt, k, vt)


def _retention_consts():
    C, H = RET_CHUNK, RET_HEADS
    log_g = jnp.log(1.0 - jnp.exp2(-5.0 - jnp.arange(H, dtype=F32)))
    idx = jnp.arange(C, dtype=F32)
    diff = idx[:, None] - idx[None, :]
    decay = jnp.where(diff >= 0, jnp.exp(log_g[:, None, None] * jnp.maximum(diff, 0.0)), 0.0)
    q_decay = jnp.exp(log_g[:, None] * (idx + 1.0))
    k_decay = jnp.exp(log_g[:, None] * (C - 1.0 - idx))
    chunk_decay = jnp.exp(log_g * C)
    qd = jnp.repeat(q_decay.T, RET_DK, axis=1)
    kd = jnp.repeat(k_decay.T, RET_DK, axis=1)
    cd = jnp.broadcast_to(jnp.repeat(chunk_decay, RET_DK)[:, None], (H * RET_DK, RET_WIDTH))
    return decay, qd, kd, cd


def _retention_kernel(z_ref, cos_ref, sin_ref, dec_ref, qd_ref, kd_ref, cd_ref, gn_ref, o_ref,
                      *, n_chunks, group_size):
    C = RET_CHUNK
    lane = lax.broadcasted_iota(jnp.int32, (C, LANES), 1)
    first_half = (lane % 32) < 16
    low = lane < RET_DV
    head_of_lane = lane // RET_DK
    srow = lax.broadcasted_iota(jnp.int32, (LANES, RET_WIDTH), 0) // RET_DK
    scol = lax.broadcasted_iota(jnp.int32, (LANES, RET_WIDTH), 1) // RET_DV
    same_head = srow == scol
    gn = gn_ref[...]
    cd = cd_ref[...]

    def rotate(x, rows):
        return x * cos_ref[rows, :] + _swap_halves(x, first_half) * sin_ref[rows, :]

    def group(n, state):
        rows = [pl.ds(pl.multiple_of((n * group_size + c) * C, C), C) for c in range(group_size)]
        q = [rotate(z_ref[r, 0:LANES], r) for r in rows]
        k = [rotate(z_ref[r, LANES:2 * LANES], r) * (RET_DK ** -0.5) for r in rows]
        kb = [x.astype(BF16) for x in k]
        v = [z_ref[r, 2 * LANES:2 * LANES + RET_WIDTH].astype(BF16) for r in rows]

        s = [[(lax.dot_general(jnp.where(head_of_lane == h, q[c], 0.0).astype(BF16), kb[c],
                               (((1,), (1,)), ((), ())), preferred_element_type=F32)
               * dec_ref[h]).astype(BF16) for h in range(RET_HEADS)] for c in range(group_size)]
        inner = [[jnp.where(low,
                            jnp.dot(s[c][2 * p], v[c][:, p * LANES:(p + 1) * LANES],
                                    preferred_element_type=F32),
                            jnp.dot(s[c][2 * p + 1], v[c][:, p * LANES:(p + 1) * LANES],
                                    preferred_element_type=F32))
                  for p in range(RET_HEADS // 2)] for c in range(group_size)]
        kv = [lax.dot_general((k[c] * kd_ref[...]).astype(BF16), v[c], (((0,), (0,)), ((), ())),
                              preferred_element_type=F32) for c in range(group_size)]
        states = [state]
        for c in range(group_size):
            states.append(states[c] * cd + jnp.where(same_head, kv[c], 0.0))
        cross = [jnp.dot((q[c] * qd_ref[...]).astype(BF16), states[c].astype(BF16),
                         preferred_element_type=F32) for c in range(group_size)]

        for c in range(group_size):
            for p in range(RET_HEADS // 2):
                cols = slice(p * LANES, (p + 1) * LANES)
                o = inner[c][p] + cross[c][:, cols]
                s_lo = jnp.sum(jnp.where(low, o, 0.0), axis=-1, keepdims=True)
                s_hi = jnp.sum(jnp.where(low, 0.0, o), axis=-1, keepdims=True)
                d = o - jnp.where(low, s_lo, s_hi) * (1.0 / RET_DV)
                d2 = d * d
                v_lo = jnp.sum(jnp.where(low, d2, 0.0), axis=-1, keepdims=True)
                v_hi = jnp.sum(jnp.where(low, 0.0, d2), axis=-1, keepdims=True)
                var = jnp.where(low, v_lo, v_hi) * (1.0 / RET_DV)
                y = d * lax.rsqrt(var + NORM_EPS) * gn[:, cols]
                gp = z_ref[rows[c], 2 * LANES + RET_WIDTH + p * LANES:
                           2 * LANES + RET_WIDTH + (p + 1) * LANES]
                o_ref[rows[c], cols] = (gp * jax.nn.sigmoid(gp) * y).astype(BF16)
        return states[group_size]

    lax.fori_loop(0, n_chunks // group_size, group, jnp.zeros((LANES, RET_WIDTH), F32))


def _retention(zr, cos, sin, consts, gn, l):
    B, S, _ = zr.shape
    decay, qd, kd, cd = consts
    full = lambda a: pl.BlockSpec(a.shape, lambda b: (0,) * a.ndim)
    return pl.pallas_call(
        functools.partial(_retention_kernel, n_chunks=S // RET_CHUNK,
                          group_size=RET_GROUP),
        grid=(B,),
        in_specs=[pl.BlockSpec((None, S, Z_RET), lambda b: (b, 0, 0)),
                  pl.BlockSpec((None, S, LANES), lambda b: (b, 0, 0)),
                  pl.BlockSpec((None, S, LANES), lambda b: (b, 0, 0)),
                  full(decay), full(qd), full(kd), full(cd),
                  pl.BlockSpec((None, 1, RET_WIDTH), lambda b: (l, 0, 0))],
        out_specs=pl.BlockSpec((None, S, RET_WIDTH), lambda b: (b, 0, 0)),
        out_shape=jax.ShapeDtypeStruct((B, S, RET_WIDTH), BF16),
        compiler_params=_params("parallel"),
        name="retention",
    )(zr, cos, sin, decay, qd, kd, cd, gn)


def _softplus(x):
    return jnp.maximum(x, 0.0) + jnp.log1p(jnp.exp(-jnp.abs(x)))


def _gelu_tanh(x):
    return 0.5 * x * (1.0 + jnp.tanh(0.7978845608028654 * (x + 0.044715 * (x * x * x))))


def _lru_kernel(z_ref, cw_ref, cb_ref, wa_ref, ba_ref, wi_ref, bi_ref, lam_ref, o_ref,
                xbuf, a_s, b_s, h_s, *, ts):
    W = LRU_WIDTH
    B = z_ref.shape[1]
    PAD = 8

    @pl.when(pl.program_id(0) == 0)
    def _():
        xbuf[0:PAD] = jnp.zeros((PAD, B, W), F32)
        h_s[...] = jnp.zeros((B, W), F32)

    xbuf[PAD:PAD + ts] = z_ref[:, :, 0:W]
    cw = cw_ref[...]
    xc = cb_ref[...].reshape(1, 1, W) + sum(
        xbuf[PAD - (CONV_WIDTH - 1) + j:PAD - (CONV_WIDTH - 1) + j + ts] * cw[j].reshape(1, 1, W)
        for j in range(CONV_WIDTH))
    xbuf[PAD - (CONV_WIDTH - 1):PAD] = xbuf[PAD + ts - (CONV_WIDTH - 1):PAD + ts]

    xc2 = xc.reshape(ts * B, W)
    xb = xc2.astype(BF16)
    r = jax.nn.sigmoid(jnp.dot(xb, wa_ref[...], preferred_element_type=F32) + ba_ref[...])
    i = jax.nn.sigmoid(jnp.dot(xb, wi_ref[...], preferred_element_type=F32) + bi_ref[...])
    log_a = (-LRU_C * _softplus(-lam_ref[...])) * r
    a = jnp.exp(log_a)
    a_s[...] = a.reshape(ts, B, W)
    one_minus_a2 = -jnp.tanh(log_a) * (a * a + 1.0)
    b_s[...] = (jnp.sqrt(one_minus_a2) * i * xc2).reshape(ts, B, W)

    def step(t, h):
        h = a_s[t] * h + b_s[t]
        b_s[t] = h
        return h

    h_s[...] = lax.fori_loop(0, ts, step, h_s[...], unroll=8)
    o_ref[...] = (_gelu_tanh(z_ref[:, :, W:]) * b_s[...]).astype(BF16)


def _rg_lru(zl, cw, cb, wa, ba, wi, bi, lam, l, ts):
    S, B, _ = zl.shape
    W = LRU_WIDTH
    vec = pl.BlockSpec((None, 1, W), lambda i: (l, 0, 0))
    mat = pl.BlockSpec((None, W, W), lambda i: (l, 0, 0))
    return pl.pallas_call(
        functools.partial(_lru_kernel, ts=ts),
        grid=(S // ts,),
        in_specs=[pl.BlockSpec((ts, B, Z_LRU), lambda i: (i, 0, 0)),
                  pl.BlockSpec((None, CONV_WIDTH, W), lambda i: (l, 0, 0)),
                  vec, mat, vec, mat, vec, vec],
        out_specs=pl.BlockSpec((ts, B, W), lambda i: (i, 0, 0)),
        out_shape=jax.ShapeDtypeStruct((S, B, W), BF16),
        scratch_shapes=[pltpu.VMEM((ts + 8, B, W), F32), pltpu.VMEM((ts, B, W), F32),
                        pltpu.VMEM((ts, B, W), F32), pltpu.VMEM((B, W), F32)],
        compiler_params=_params("arbitrary"),
        name="rg_lru",
    )(zl, cw, cb, wa, ba, wi, bi, lam)


def _out_ffn_kernel(x_ref, ya_ref, yb_ref, yc_ref, mod_ref, g_ref, wo_ref, wg_ref, wu_ref,
                    wd_ref, fmod_ref, fg_ref, o_ref, h_s, acc_s, *, final):
    mod = mod_ref[...]
    a0, b0 = MLA_WIDTH, MLA_WIDTH + RET_WIDTH
    y = (jnp.dot(ya_ref[...], wo_ref[0:a0, :], preferred_element_type=F32)
         + jnp.dot(yb_ref[...], wo_ref[a0:b0, :], preferred_element_type=F32)
         + jnp.dot(yc_ref[...], wo_ref[b0:, :], preferred_element_type=F32))
    x1 = x_ref[...] + mod[2:3] * y
    o_ref[...] = x1
    h_s[...] = (_rms(x1, g_ref[...]) * (1.0 + mod[4:5]) + mod[3:4]).astype(BF16)

    def gate_up(j):
        cols = slice(j * FFN_CHUNK, (j + 1) * FFN_CHUNK)
        h = h_s[...]
        return (jnp.dot(h, wg_ref[:, cols], preferred_element_type=F32),
                jnp.dot(h, wu_ref[:, cols], preferred_element_type=F32))

    n_chunks = FFN_HIDDEN // FFN_CHUNK
    nxt = gate_up(0)
    for j in range(n_chunks):
        gate, up = nxt
        if j + 1 < n_chunks:
            nxt = gate_up(j + 1)
        act = (gate * jax.nn.sigmoid(gate) * up).astype(BF16)
        down = jnp.dot(act, wd_ref[j * FFN_CHUNK:(j + 1) * FFN_CHUNK, :],
                       preferred_element_type=F32)
        acc_s[...] = down if j == 0 else acc_s[...] + down
    x2 = o_ref[...] + mod[5:6] * acc_s[...]
    if final:
        fmod = fmod_ref[...]
        x2 = _rms(x2, fg_ref[...]) * (1.0 + fmod[1:2]) + fmod[0:1]
    o_ref[...] = x2


def _out_ffn(x, ya, yb, yc, mod, norm, wo, wgu, wd, fmod, fnorm, l, tm, final):
    B, S, D = x.shape
    tok = lambda w: pl.BlockSpec((None, tm, w), lambda b, i: (b, i, 0))
    return pl.pallas_call(
        functools.partial(_out_ffn_kernel, final=final),
        grid=(B, S // tm),
        in_specs=[tok(D), tok(MLA_WIDTH), tok(RET_WIDTH),
                  pl.BlockSpec((tm, LRU_WIDTH), lambda b, i: (i, b)),
                  pl.BlockSpec((None, None, N_MOD, D), lambda b, i: (l, b, 0, 0)),
                  pl.BlockSpec((None, 1, D), lambda b, i: (l, 0, 0)),
                  _resident((None, D, D), lambda b, i: (l, 0, 0)),
                  _resident((None, D, FFN_HIDDEN), lambda b, i: (l, 0, 0)),
                  _resident((None, D, FFN_HIDDEN), lambda b, i: (l, 0, 1)),
                  _resident((None, FFN_HIDDEN, D), lambda b, i: (l, 0, 0)),
                  pl.BlockSpec((None, 2, D), lambda b, i: (b, 0, 0)),
                  pl.BlockSpec((1, D), lambda b, i: (0, 0))],
        out_specs=tok(D),
        out_shape=jax.ShapeDtypeStruct((B, S, D), F32),
        scratch_shapes=[pltpu.VMEM((tm, D), BF16), pltpu.VMEM((tm, D), F32)],
        compiler_params=_params("parallel", "parallel"),
        name="out_ffn",
    )(x, ya, yb, yc, mod, norm, wo, wgu, wgu, wd, fmod, fnorm)


def _pad_in_proj(w_in):
    n_mla = MLA_Q_RANK + MLA_KV_RANK + MLA_ROPE
    pad = jnp.zeros(w_in.shape[:2] + (Z_MLA - n_mla,), w_in.dtype)
    return jnp.concatenate([w_in[..., :n_mla], pad, w_in[..., n_mla:]], axis=-1).astype(BF16)


def _pad_heads(w, width):
    L, R, H, _ = w.shape
    return jnp.pad(w, ((0, 0), (0, 0), (0, 0), (0, HEAD_PAD - width))).reshape(L, R, H * HEAD_PAD)


def _mla_weights(w_uq, w_ukv):
    L = w_uq.shape[0]
    wq = _pad_heads(w_uq.reshape(L, MLA_Q_RANK, MLA_HEADS, MLA_NOPE + MLA_ROPE), MLA_NOPE + MLA_ROPE)
    kv = w_ukv.reshape(L, MLA_KV_RANK, MLA_HEADS, MLA_NOPE + MLA_V)
    wk = _pad_heads(kv[..., :MLA_NOPE], MLA_NOPE)
    v = kv[..., MLA_NOPE:]
    odd = (jnp.arange(MLA_HEADS) % 2 == 1)[None, None, :, None]
    zero = jnp.zeros_like(v)
    wv = jnp.concatenate([jnp.where(odd, zero, v), jnp.where(odd, v, zero)], axis=-1)
    wv = wv.reshape(L, MLA_KV_RANK, MLA_PAD)
    r = jnp.arange(LANES)[:, None]
    c = jnp.arange(MLA_PAD)[None, :]
    wr = ((r < MLA_ROPE) & (c % HEAD_PAD == MLA_NOPE + r)).astype(BF16)
    ones = ((c % HEAD_PAD) == jnp.where((c // HEAD_PAD) % 2 == 0, MLA_V, 0)).astype(F32)
    return wq.astype(BF16), wk.astype(BF16), wr, wv.astype(BF16), ones


def _block_diag(w):
    L, G, I, J = w.shape
    eye = jnp.eye(G, dtype=w.dtype)
    return jnp.einsum('lgij,gh->lgihj', w, eye).reshape(L, G * I, G * J)


def kernel(x, c, positions, mod_w, mod_b, norm1, w_in, mla_q_norm, mla_w_uq, mla_kv_norm, mla_w_ukv, ret_gn, lru_conv_w, lru_conv_b, lru_w_a, lru_b_a, lru_w_i, lru_b_i, lru_lambda, w_out, norm2, w_gate_up, w_down, final_norm, final_mod_w, final_mod_b):
    B, S, D = x.shape
    L = mod_w.shape[0]
    tm = min(S, 512)
    tq = min(S, 256)
    ts = min(S, 128)
    row = lambda a: a.reshape(L, 1, a.shape[-1])

    mod = _modulation(c, mod_w, mod_b, 1536).reshape(L, B, N_MOD, D)
    fmod = _modulation(c, final_mod_w[None], final_mod_b[None], 1024).reshape(B, 2, D)
    fnorm = final_norm.reshape(1, D)
    cos, sin = _rope_tables(positions)

    w_in_p = _pad_in_proj(w_in)
    wq, wk, wr, wv, ones = _mla_weights(mla_w_uq, mla_w_ukv)
    wa = _block_diag(lru_w_a).astype(BF16)
    wi = _block_diag(lru_w_i).astype(BF16)
    wo = w_out.astype(BF16)
    wgu = w_gate_up.astype(BF16)
    wd = w_down.astype(BF16)
    ret_consts = _retention_consts()

    for l in range(L):
        zm, zr, zl = _in_proj(x, mod, row(norm1), w_in_p, l, tm)
        qt, k, vt = _mla_prep(zm, cos, sin, row(mla_q_norm), row(mla_kv_norm), wq, wk, wr, wv,
                              ones, l, tm, tq)
        ya = _attention(qt, k, vt, tq)
        yb = _retention(zr, cos, sin, ret_consts, row(ret_gn), l)
        yc = _rg_lru(zl.reshape(S, B, Z_LRU), lru_conv_w, row(lru_conv_b), wa, row(lru_b_a),
                     wi, row(lru_b_i), row(lru_lambda), l, ts)
        x = _out_ffn(x, ya, yb, yc.reshape(S, B * LRU_WIDTH), mod, row(norm2), wo, wgu, wd,
                     fmod, fnorm, l, tm, final=(l == L - 1))
    return x
```

```python
import functools

import jax
import jax.numpy as jnp
from jax import lax
from jax.experimental import pallas as pl
from jax.experimental.pallas import tpu as pltpu

D_MODEL = 1024
MLA_HEADS = 6
MLA_Q_RANK = 256
MLA_KV_RANK = 128
MLA_NOPE = 64
MLA_ROPE = 32
MLA_V = 64
MLA_WIDTH = MLA_HEADS * MLA_V
RET_HEADS = 4
RET_DK = 32
RET_DV = 64
RET_WIDTH = RET_HEADS * RET_DV
RET_CHUNK = 128
RET_GROUP = 4
LRU_WIDTH = D_MODEL - MLA_WIDTH - RET_WIDTH
LRU_BLOCKS = 6
LRU_BLOCK = LRU_WIDTH // LRU_BLOCKS
CONV_WIDTH = 4
LRU_C = 8.0
FFN_HIDDEN = 2816
ROPE_BASE = 10000.0
NORM_EPS = 1e-6
LOG2_E = 1.4426950408889634
N_MOD = 6

LANES = 128
HEAD_PAD = LANES
MLA_PAD = MLA_HEADS * HEAD_PAD
Z_MLA = 512
Z_RET = 2 * RET_HEADS * RET_DK + 2 * RET_WIDTH
Z_LRU = 2 * LRU_WIDTH
Z_ALL = Z_MLA + Z_RET + Z_LRU
FFN_CHUNK = 256
VMEM_LIMIT = 56 * 1024 * 1024

BF16 = jnp.bfloat16
F32 = jnp.float32


def _params(*sem):
    return pltpu.CompilerParams(dimension_semantics=sem, vmem_limit_bytes=VMEM_LIMIT)


def _resident(shape, index_map):
    return pl.BlockSpec(shape, index_map, pipeline_mode=pl.Buffered(1))


def _rms(x, g):
    return x * lax.rsqrt(jnp.mean(x * x, axis=-1, keepdims=True) + NORM_EPS) * g


def _swap_halves(x, first_half):
    return jnp.where(first_half, pltpu.roll(x, LANES - 16, 1), pltpu.roll(x, 16, 1))


def _mod_kernel(c_ref, w_ref, b_ref, o_ref):
    c = c_ref[...]
    ca = (c * jax.nn.sigmoid(c)).astype(BF16)
    o_ref[...] = jnp.dot(ca, w_ref[...].astype(BF16), preferred_element_type=F32) + b_ref[...]


def _modulation(c, w, b, tn):
    L, D, N = w.shape
    B = c.shape[0]
    return pl.pallas_call(
        _mod_kernel,
        grid=(L, N // tn),
        in_specs=[pl.BlockSpec((B, D), lambda l, j: (0, 0)),
                  pl.BlockSpec((None, D, tn), lambda l, j: (l, 0, j)),
                  pl.BlockSpec((None, 1, tn), lambda l, j: (l, 0, j))],
        out_specs=pl.BlockSpec((None, B, tn), lambda l, j: (l, 0, j)),
        out_shape=jax.ShapeDtypeStruct((L, B, N), F32),
        compiler_params=_params("parallel", "parallel"),
        name="modulation",
    )(c, w, b.reshape(L, 1, N))


def _rope_table_kernel(pos_ref, posr_ref, inv_ref, invc_ref, cos_ref, sin_ref, cost_ref, sint_ref):
    ang = pos_ref[...].astype(F32) * inv_ref[...]
    lane = lax.broadcasted_iota(jnp.int32, ang.shape, 1)
    first_half = (lane % 32) < 16
    cos_ref[...] = jnp.cos(ang)
    sin = jnp.sin(ang)
    sin_ref[...] = jnp.where(first_half, -sin, sin)
    ang_t = invc_ref[...] * posr_ref[...].astype(F32)
    cost_ref[...] = jnp.cos(ang_t)
    sint_ref[...] = jnp.sin(ang_t)


def _rope_tables(positions):
    B, S = positions.shape
    half = MLA_ROPE // 2
    inv = ROPE_BASE ** (-jnp.arange(half, dtype=F32) / half)
    ts = min(S, 1024)
    spec = pl.BlockSpec((None, ts, LANES), lambda b, i: (b, i, 0))
    spec_t = pl.BlockSpec((None, half, ts), lambda b, i: (b, 0, i))
    return pl.pallas_call(
        _rope_table_kernel,
        grid=(B, S // ts),
        in_specs=[pl.BlockSpec((None, ts, 1), lambda b, i: (b, i, 0)),
                  pl.BlockSpec((None, 1, ts), lambda b, i: (b, 0, i)),
                  pl.BlockSpec((1, LANES), lambda b, i: (0, 0)),
                  pl.BlockSpec((half, 1), lambda b, i: (0, 0))],
        out_specs=[spec, spec, spec_t, spec_t],
        out_shape=[jax.ShapeDtypeStruct((B, S, LANES), F32)] * 2
        + [jax.ShapeDtypeStruct((B, half, S), F32)] * 2,
        compiler_params=_params("parallel", "parallel"),
        name="rope_tables",
    )(positions.reshape(B, S, 1), positions.reshape(B, 1, S),
      jnp.tile(inv, LANES // half).reshape(1, LANES), inv.reshape(half, 1))


def _in_proj_kernel(x_ref, mod_ref, g_ref, w_ref, zm_ref, zr_ref, zl_ref):
    B, tt, D = x_ref.shape
    mod = mod_ref[...]
    h = _rms(x_ref[...], g_ref[...]) * (1.0 + mod[:, 1:2]) + mod[:, 0:1]
    z = jnp.dot(h.reshape(B * tt, D).astype(BF16), w_ref[...], preferred_element_type=F32)
    zm_ref[...] = z[:, :Z_MLA].reshape(B, tt, Z_MLA)
    zr_ref[...] = z[:, Z_MLA:Z_MLA + Z_RET].reshape(B, tt, Z_RET)
    zl_ref[...] = pltpu.einshape("btd->tbd", z[:, Z_MLA + Z_RET:].reshape(B, tt, Z_LRU))


def _in_proj(x, mod, norm, w, l, tt):
    B, S, D = x.shape
    tok = lambda width: pl.BlockSpec((B, tt, width), lambda i: (0, i, 0))
    return pl.pallas_call(
        _in_proj_kernel,
        grid=(S // tt,),
        in_specs=[tok(D),
                  pl.BlockSpec((None, B, N_MOD, D), lambda i: (l, 0, 0, 0)),
                  pl.BlockSpec((None, 1, D), lambda i: (l, 0, 0)),
                  _resident((None, D, Z_ALL), lambda i: (l, 0, 0))],
        out_specs=[tok(Z_MLA), tok(Z_RET),
                   pl.BlockSpec((tt, B, Z_LRU), lambda i: (i, 0, 0))],
        out_shape=[jax.ShapeDtypeStruct((B, S, Z_MLA), F32),
                   jax.ShapeDtypeStruct((B, S, Z_RET), F32),
                   jax.ShapeDtypeStruct((S, B, Z_LRU), F32)],
        compiler_params=_params("parallel"),
        name="in_proj",
    )(x, mod, norm, w)


def _rms_feature_major(x, g):
    return x * lax.rsqrt(jnp.mean(x * x, axis=0, keepdims=True) + NORM_EPS) * g


def _mla_prep_kernel(z_ref, cos_ref, sin_ref, cost_ref, sint_ref, qn_ref, kvn_ref, kvnr_ref,
                     wqt_ref, wk_ref, wr_ref, wvt_ref, one_ref, qt_ref, k_ref, vt_ref, *, tq):
    z = z_ref[...]
    cos = cos_ref[...]
    sin = sin_ref[...]
    lane = lax.broadcasted_iota(jnp.int32, cos.shape, 1)
    first_half = (lane % 32) < 16

    c_kv = _rms(z[:, MLA_Q_RANK:MLA_Q_RANK + MLA_KV_RANK], kvnr_ref[...]).astype(BF16)
    kr = z[:, MLA_Q_RANK + MLA_KV_RANK:]
    kr = jnp.where(lane < MLA_ROPE, kr * cos + _swap_halves(kr, first_half) * sin, 0.0)
    k = (jnp.dot(c_kv, wk_ref[...], preferred_element_type=F32)
         + jnp.dot(kr.astype(BF16), wr_ref[...], preferred_element_type=F32))
    k_ref[...] = k.astype(BF16)

    zt = z[:, :MLA_Q_RANK + MLA_KV_RANK].T
    c_q = _rms_feature_major(zt[:MLA_Q_RANK], qn_ref[...]).astype(BF16)
    q = jnp.dot(wqt_ref[...], c_q, preferred_element_type=F32)
    qk_scale = (MLA_NOPE + MLA_ROPE) ** -0.5 * LOG2_E
    cos_t = cost_ref[...] * qk_scale
    sin_t = sint_ref[...] * qk_scale
    half = MLA_ROPE // 2
    rows = []
    for h in range(MLA_HEADS):
        r0 = h * HEAD_PAD + MLA_NOPE
        x1 = q[r0:r0 + half]
        x2 = q[r0 + half:r0 + MLA_ROPE]
        rows += [q[h * HEAD_PAD:r0] * qk_scale, x1 * cos_t - x2 * sin_t, x2 * cos_t + x1 * sin_t,
                 q[r0 + MLA_ROPE:(h + 1) * HEAD_PAD]]
    q = jnp.concatenate(rows, axis=0).astype(BF16)
    c_kv_t = _rms_feature_major(zt[MLA_Q_RANK:], kvn_ref[...]).astype(BF16)
    v = (jnp.dot(wvt_ref[...], c_kv_t, preferred_element_type=F32) + one_ref[...]).astype(BF16)
    for t in range(z.shape[0] // tq):
        qt_ref[t] = q[:, t * tq:(t + 1) * tq]
        vt_ref[t] = v[:, t * tq:(t + 1) * tq]


def _mla_prep(zm, tables, qn, kvn, wqt, wk, wr, wvt, ones, l, tm, tq):
    B, S, _ = zm.shape
    cos, sin, cos_t, sin_t = tables
    tok = lambda w: pl.BlockSpec((None, tm, w), lambda b, i: (b, i, 0))
    tab_t = pl.BlockSpec((None, MLA_ROPE // 2, tm), lambda b, i: (b, 0, i))
    col = lambda n: pl.BlockSpec((None, n, 1), lambda b, i: (l, 0, 0))
    tr_spec = pl.BlockSpec((None, tm // tq, MLA_PAD, tq), lambda b, i: (b, i, 0, 0))
    tr_shape = jax.ShapeDtypeStruct((B, S // tq, MLA_PAD, tq), BF16)
    L = qn.shape[0]
    return pl.pallas_call(
        functools.partial(_mla_prep_kernel, tq=tq),
        grid=(B, S // tm),
        in_specs=[tok(Z_MLA), tok(LANES), tok(LANES), tab_t, tab_t,
                  col(MLA_Q_RANK), col(MLA_KV_RANK),
                  pl.BlockSpec((None, 1, MLA_KV_RANK), lambda b, i: (l, 0, 0)),
                  _resident((None, MLA_PAD, MLA_Q_RANK), lambda b, i: (l, 0, 0)),
                  _resident((None, MLA_KV_RANK, MLA_PAD), lambda b, i: (l, 0, 0)),
                  _resident((LANES, MLA_PAD), lambda b, i: (0, 0)),
                  _resident((None, MLA_PAD, MLA_KV_RANK), lambda b, i: (l, 0, 0)),
                  pl.BlockSpec((MLA_PAD, 1), lambda b, i: (0, 0))],
        out_specs=[tr_spec, tok(MLA_PAD), tr_spec],
        out_shape=[tr_shape, jax.ShapeDtypeStruct((B, S, MLA_PAD), BF16), tr_shape],
        compiler_params=_params("parallel", "parallel"),
        name="mla_prep",
    )(zm, cos, sin, cos_t, sin_t, qn.reshape(L, -1, 1), kvn.reshape(L, -1, 1),
      kvn.reshape(L, 1, -1), wqt, wk, wr, wvt, ones)


def _attn_kernel(qt_ref, k_ref, vt_ref, o_ref, s_s, m_s, acc_s, *, tq):
    i = pl.program_id(1)
    groups = tq // 8
    key = lax.broadcasted_iota(jnp.int32, (tq, tq), 0)
    qry = lax.broadcasted_iota(jnp.int32, (tq, tq), 1)
    causal = key <= qry
    hs = lambda h: slice(h * HEAD_PAD, (h + 1) * HEAD_PAD)

    def scores(j, masked):
        rows = pl.ds(pl.multiple_of(j * tq, tq), tq)
        for h in range(MLA_HEADS):
            s = jnp.dot(k_ref[rows, hs(h)], qt_ref[hs(h), :], preferred_element_type=F32)
            if masked:
                s = jnp.where(causal, s, -jnp.inf)
            s_s[h, j] = s
            m_s[h] = jnp.maximum(m_s[h], jnp.max(s.reshape(groups, 8, tq), axis=0))

    def values(j):
        for h in range(MLA_HEADS):
            p = jnp.exp2(s_s[h, j].reshape(groups, 8, tq) - m_s[h][None])
            acc_s[h] += jnp.dot(vt_ref[j, hs(h), :], p.reshape(tq, tq).astype(BF16),
                                preferred_element_type=F32)

    def pairs(n, step):
        def body(t, carry):
            step(2 * t)
            step(2 * t + 1)
            return carry

        lax.fori_loop(0, n // 2, body, 0)

        @pl.when(n % 2 == 1)
        def _():
            step(n - 1)

    m_s[...] = jnp.full(m_s.shape, -jnp.inf, F32)
    pairs(i, functools.partial(scores, masked=False))
    scores(i, True)
    for h in range(MLA_HEADS):
        m_s[h] = jnp.broadcast_to(jnp.max(m_s[h], axis=0, keepdims=True), (8, tq))
    acc_s[...] = jnp.zeros(acc_s.shape, F32)
    pairs(i + 1, values)

    low = lax.broadcasted_iota(jnp.int32, (HEAD_PAD, tq), 0) < MLA_V
    for p in range(MLA_HEADS // 2):
        even = acc_s[2 * p]
        odd = acc_s[2 * p + 1]
        pair = jnp.where(low, even / even[MLA_V:MLA_V + 1], odd / odd[0:1])
        o_ref[:, p * LANES:(p + 1) * LANES] = pair.T.astype(BF16)


def _attention(qt, k, vt, tq):
    B, S, _ = k.shape
    nq = S // tq
    return pl.pallas_call(
        functools.partial(_attn_kernel, tq=tq),
        grid=(B, nq),
        in_specs=[pl.BlockSpec((None, None, MLA_PAD, tq), lambda b, i: (b, i, 0, 0)),
                  pl.BlockSpec((None, S, MLA_PAD), lambda b, i: (b, 0, 0)),
                  pl.BlockSpec((None, nq, MLA_PAD, tq), lambda b, i: (b, 0, 0, 0))],
        out_specs=pl.BlockSpec((None, tq, MLA_WIDTH), lambda b, i: (b, i, 0)),
        out_shape=jax.ShapeDtypeStruct((B, S, MLA_WIDTH), BF16),
        scratch_shapes=[pltpu.VMEM((MLA_HEADS, nq, tq, tq), F32),
                        pltpu.VMEM((MLA_HEADS, 8, tq), F32),
                        pltpu.VMEM((MLA_HEADS, HEAD_PAD, tq), F32)],
        compiler_params=_params("parallel", "arbitrary"),
        name="mla_attention",
    )(qt, k, vt)


def _retention_consts():
    C, H = RET_CHUNK, RET_HEADS
    log_g = jnp.log(1.0 - jnp.exp2(-5.0 - jnp.arange(H, dtype=F32)))
    idx = jnp.arange(C, dtype=F32)
    diff = idx[:, None] - idx[None, :]
    decay = jnp.where(diff >= 0, jnp.exp(log_g[:, None, None] * jnp.maximum(diff, 0.0)), 0.0)
    q_decay = jnp.exp(log_g[:, None] * (idx + 1.0))
    k_decay = jnp.exp(log_g[:, None] * (C - 1.0 - idx))
    chunk_decay = jnp.exp(log_g * C)
    qd = jnp.repeat(q_decay.T, RET_DK, axis=1)
    kd = jnp.repeat(k_decay.T, RET_DK, axis=1)
    cd = jnp.broadcast_to(jnp.repeat(chunk_decay, RET_DK)[:, None], (H * RET_DK, RET_WIDTH))
    return decay, qd, kd, cd


def _retention_kernel(z_ref, cos_ref, sin_ref, dec_ref, qd_ref, kd_ref, cd_ref, gn_ref, o_ref,
                      *, n_chunks, group_size):
    C = RET_CHUNK
    lane = lax.broadcasted_iota(jnp.int32, (C, LANES), 1)
    first_half = (lane % 32) < 16
    low = lane < RET_DV
    head_of_lane = lane // RET_DK
    srow = lax.broadcasted_iota(jnp.int32, (LANES, RET_WIDTH), 0) // RET_DK
    scol = lax.broadcasted_iota(jnp.int32, (LANES, RET_WIDTH), 1) // RET_DV
    same_head = srow == scol
    gn = gn_ref[...]
    cd = cd_ref[...]

    def rotate(x, rows):
        return x * cos_ref[rows, :] + _swap_halves(x, first_half) * sin_ref[rows, :]

    def group(n, state):
        rows = [pl.ds(pl.multiple_of((n * group_size + c) * C, C), C) for c in range(group_size)]
        q = [rotate(z_ref[r, 0:LANES], r) for r in rows]
        k = [rotate(z_ref[r, LANES:2 * LANES], r) * (RET_DK ** -0.5) for r in rows]
        kb = [x.astype(BF16) for x in k]
        v = [z_ref[r, 2 * LANES:2 * LANES + RET_WIDTH].astype(BF16) for r in rows]

        s = [[(lax.dot_general(jnp.where(head_of_lane == h, q[c], 0.0).astype(BF16), kb[c],
                               (((1,), (1,)), ((), ())), preferred_element_type=F32)
               * dec_ref[h]).astype(BF16) for h in range(RET_HEADS)] for c in range(group_size)]
        inner = [[jnp.where(low,
                            jnp.dot(s[c][2 * p], v[c][:, p * LANES:(p + 1) * LANES],
                                    preferred_element_type=F32),
                            jnp.dot(s[c][2 * p + 1], v[c][:, p * LANES:(p + 1) * LANES],
                                    preferred_element_type=F32))
                  for p in range(RET_HEADS // 2)] for c in range(group_size)]
        kv = [lax.dot_general((k[c] * kd_ref[...]).astype(BF16), v[c], (((0,), (0,)), ((), ())),
                              preferred_element_type=F32) for c in range(group_size)]
        states = [state]
        for c in range(group_size):
            states.append(states[c] * cd + jnp.where(same_head, kv[c], 0.0))
        cross = [jnp.dot((q[c] * qd_ref[...]).astype(BF16), states[c].astype(BF16),
                         preferred_element_type=F32) for c in range(group_size)]

        for c in range(group_size):
            for p in range(RET_HEADS // 2):
                cols = slice(p * LANES, (p + 1) * LANES)
                o = inner[c][p] + cross[c][:, cols]
                s_lo = jnp.sum(jnp.where(low, o, 0.0), axis=-1, keepdims=True)
                s_hi = jnp.sum(jnp.where(low, 0.0, o), axis=-1, keepdims=True)
                d = o - jnp.where(low, s_lo, s_hi) * (1.0 / RET_DV)
                d2 = d * d
                v_lo = jnp.sum(jnp.where(low, d2, 0.0), axis=-1, keepdims=True)
                v_hi = jnp.sum(jnp.where(low, 0.0, d2), axis=-1, keepdims=True)
                var = jnp.where(low, v_lo, v_hi) * (1.0 / RET_DV)
                y = d * lax.rsqrt(var + NORM_EPS) * gn[:, cols]
                gp = z_ref[rows[c], 2 * LANES + RET_WIDTH + p * LANES:
                           2 * LANES + RET_WIDTH + (p + 1) * LANES]
                o_ref[rows[c], cols] = (gp * jax.nn.sigmoid(gp) * y).astype(BF16)
        return states[group_size]

    lax.fori_loop(0, n_chunks // group_size, group, jnp.zeros((LANES, RET_WIDTH), F32))


def _retention(zr, cos, sin, consts, gn, l):
    B, S, _ = zr.shape
    decay, qd, kd, cd = consts
    full = lambda a: pl.BlockSpec(a.shape, lambda b: (0,) * a.ndim)
    return pl.pallas_call(
        functools.partial(_retention_kernel, n_chunks=S // RET_CHUNK,
                          group_size=RET_GROUP),
        grid=(B,),
        in_specs=[pl.BlockSpec((None, S, Z_RET), lambda b: (b, 0, 0)),
                  pl.BlockSpec((None, S, LANES), lambda b: (b, 0, 0)),
                  pl.BlockSpec((None, S, LANES), lambda b: (b, 0, 0)),
                  full(decay), full(qd), full(kd), full(cd),
                  pl.BlockSpec((None, 1, RET_WIDTH), lambda b: (l, 0, 0))],
        out_specs=pl.BlockSpec((None, S, RET_WIDTH), lambda b: (b, 0, 0)),
        out_shape=jax.ShapeDtypeStruct((B, S, RET_WIDTH), BF16),
        compiler_params=_params("parallel"),
        name="retention",
    )(zr, cos, sin, decay, qd, kd, cd, gn)


def _softplus(x):
    return jnp.maximum(x, 0.0) + jnp.log1p(jnp.exp(-jnp.abs(x)))


def _gelu_tanh(x):
    return 0.5 * x * (1.0 + jnp.tanh(0.7978845608028654 * (x + 0.044715 * (x * x * x))))


def _lru_kernel(z_ref, cw_ref, cb_ref, wa_ref, ba_ref, wi_ref, bi_ref, lam_ref, o_ref,
                xbuf, a_s, b_s, h_s, *, ts):
    W = LRU_WIDTH
    B = z_ref.shape[1]
    PAD = 8

    @pl.when(pl.program_id(0) == 0)
    def _():
        xbuf[0:PAD] = jnp.zeros((PAD, B, W), F32)
        h_s[...] = jnp.zeros((B, W), F32)

    xbuf[PAD:PAD + ts] = z_ref[:, :, 0:W]
    cw = cw_ref[...]
    xc = cb_ref[...].reshape(1, 1, W) + sum(
        xbuf[PAD - (CONV_WIDTH - 1) + j:PAD - (CONV_WIDTH - 1) + j + ts] * cw[j].reshape(1, 1, W)
        for j in range(CONV_WIDTH))
    xbuf[PAD - (CONV_WIDTH - 1):PAD] = xbuf[PAD + ts - (CONV_WIDTH - 1):PAD + ts]

    xc2 = xc.reshape(ts * B, W)
    xb = xc2.astype(BF16)
    r = jax.nn.sigmoid(jnp.dot(xb, wa_ref[...], preferred_element_type=F32) + ba_ref[...])
    i = jax.nn.sigmoid(jnp.dot(xb, wi_ref[...], preferred_element_type=F32) + bi_ref[...])
    log_a = (-LRU_C * _softplus(-lam_ref[...])) * r
    a = jnp.exp(log_a)
    a_s[...] = a.reshape(ts, B, W)
    one_minus_a2 = -jnp.tanh(log_a) * (a * a + 1.0)
    b_s[...] = (jnp.sqrt(one_minus_a2) * i * xc2).reshape(ts, B, W)

    def step(t, h):
        h = a_s[t] * h + b_s[t]
        b_s[t] = h
        return h

    h_s[...] = lax.fori_loop(0, ts, step, h_s[...], unroll=8)
    o_ref[...] = _gelu_tanh(z_ref[:, :, W:]) * b_s[...]


def _rg_lru(zl, cw, cb, wa, ba, wi, bi, lam, l, ts):
    S, B, _ = zl.shape
    W = LRU_WIDTH
    vec = pl.BlockSpec((None, 1, W), lambda i: (l, 0, 0))
    mat = pl.BlockSpec((None, W, W), lambda i: (l, 0, 0))
    return pl.pallas_call(
        functools.partial(_lru_kernel, ts=ts),
        grid=(S // ts,),
        in_specs=[pl.BlockSpec((ts, B, Z_LRU), lambda i: (i, 0, 0)),
                  pl.BlockSpec((None, CONV_WIDTH, W), lambda i: (l, 0, 0)),
                  vec, mat, vec, mat, vec, vec],
        out_specs=pl.BlockSpec((ts, B, W), lambda i: (i, 0, 0)),
        out_shape=jax.ShapeDtypeStruct((S, B, W), F32),
        scratch_shapes=[pltpu.VMEM((ts + 8, B, W), F32), pltpu.VMEM((ts, B, W), F32),
                        pltpu.VMEM((ts, B, W), F32), pltpu.VMEM((B, W), F32)],
        compiler_params=_params("arbitrary"),
        name="rg_lru",
    )(zl, cw, cb, wa, ba, wi, bi, lam)


def _out_ffn_kernel(x_ref, ya_ref, yb_ref, yc_ref, mod_ref, g_ref, wo_ref, wg_ref, wu_ref,
                    wd_ref, fmod_ref, fg_ref, o_ref, h_s, acc_s, *, final):
    B, tt, D = x_ref.shape
    rows = B * tt
    mod = mod_ref[...]
    a0, b0 = MLA_WIDTH, MLA_WIDTH + RET_WIDTH
    yc = pltpu.einshape("tbd->btd", yc_ref[...]).reshape(rows, LRU_WIDTH).astype(BF16)
    y = (jnp.dot(ya_ref[...].reshape(rows, MLA_WIDTH), wo_ref[0:a0, :],
                 preferred_element_type=F32)
         + jnp.dot(yb_ref[...].reshape(rows, RET_WIDTH), wo_ref[a0:b0, :],
                   preferred_element_type=F32)
         + jnp.dot(yc, wo_ref[b0:, :], preferred_element_type=F32))
    x1 = x_ref[...] + mod[:, 2:3] * y.reshape(B, tt, D)
    o_ref[...] = x1
    h = _rms(x1, g_ref[...]) * (1.0 + mod[:, 4:5]) + mod[:, 3:4]
    h_s[...] = h.reshape(rows, D).astype(BF16)

    def gate_up(j):
        cols = slice(j * FFN_CHUNK, (j + 1) * FFN_CHUNK)
        h = h_s[...]
        return (jnp.dot(h, wg_ref[:, cols], preferred_element_type=F32),
                jnp.dot(h, wu_ref[:, cols], preferred_element_type=F32))

    n_chunks = FFN_HIDDEN // FFN_CHUNK
    nxt = gate_up(0)
    for j in range(n_chunks):
        gate, up = nxt
        if j + 1 < n_chunks:
            nxt = gate_up(j + 1)
        act = (gate * jax.nn.sigmoid(gate) * up).astype(BF16)
        down = jnp.dot(act, wd_ref[j * FFN_CHUNK:(j + 1) * FFN_CHUNK, :],
                       preferred_element_type=F32)
        acc_s[...] = down if j == 0 else acc_s[...] + down
    x2 = o_ref[...] + mod[:, 5:6] * acc_s[...].reshape(B, tt, D)
    if final:
        fmod = fmod_ref[...]
        x2 = _rms(x2, fg_ref[...]) * (1.0 + fmod[:, 1:2]) + fmod[:, 0:1]
    o_ref[...] = x2


def _out_ffn(x, ya, yb, yc, mod, norm, wo, wgu, wd, fmod, fnorm, l, tt, final):
    B, S, D = x.shape
    tok = lambda width: pl.BlockSpec((B, tt, width), lambda i: (0, i, 0))
    return pl.pallas_call(
        functools.partial(_out_ffn_kernel, final=final),
        grid=(S // tt,),
        in_specs=[tok(D), tok(MLA_WIDTH), tok(RET_WIDTH),
                  pl.BlockSpec((tt, B, LRU_WIDTH), lambda i: (i, 0, 0)),
                  pl.BlockSpec((None, B, N_MOD, D), lambda i: (l, 0, 0, 0)),
                  pl.BlockSpec((None, 1, D), lambda i: (l, 0, 0)),
                  _resident((None, D, D), lambda i: (l, 0, 0)),
                  _resident((None, D, FFN_HIDDEN), lambda i: (l, 0, 0)),
                  _resident((None, D, FFN_HIDDEN), lambda i: (l, 0, 1)),
                  _resident((None, FFN_HIDDEN, D), lambda i: (l, 0, 0)),
                  pl.BlockSpec((B, 2, D), lambda i: (0, 0, 0)),
                  pl.BlockSpec((1, D), lambda i: (0, 0))],
        out_specs=tok(D),
        out_shape=jax.ShapeDtypeStruct((B, S, D), F32),
        scratch_shapes=[pltpu.VMEM((B * tt, D), BF16), pltpu.VMEM((B * tt, D), F32)],
        compiler_params=_params("parallel"),
        name="out_ffn",
    )(x, ya, yb, yc, mod, norm, wo, wgu, wgu, wd, fmod, fnorm)


def _pad_in_proj(w_in):
    n_mla = MLA_Q_RANK + MLA_KV_RANK + MLA_ROPE
    pad = jnp.zeros(w_in.shape[:2] + (Z_MLA - n_mla,), w_in.dtype)
    return jnp.concatenate([w_in[..., :n_mla], pad, w_in[..., n_mla:]], axis=-1).astype(BF16)


def _pad_heads(w, width):
    L, R, H, _ = w.shape
    return jnp.pad(w, ((0, 0), (0, 0), (0, 0), (0, HEAD_PAD - width))).reshape(L, R, H * HEAD_PAD)


def _mla_weights(w_uq, w_ukv):
    L = w_uq.shape[0]
    wq = _pad_heads(w_uq.reshape(L, MLA_Q_RANK, MLA_HEADS, MLA_NOPE + MLA_ROPE), MLA_NOPE + MLA_ROPE)
    kv = w_ukv.reshape(L, MLA_KV_RANK, MLA_HEADS, MLA_NOPE + MLA_V)
    wk = _pad_heads(kv[..., :MLA_NOPE], MLA_NOPE)
    v = kv[..., MLA_NOPE:]
    odd = (jnp.arange(MLA_HEADS) % 2 == 1)[None, None, :, None]
    zero = jnp.zeros_like(v)
    wv = jnp.concatenate([jnp.where(odd, zero, v), jnp.where(odd, v, zero)], axis=-1)
    wv = wv.reshape(L, MLA_KV_RANK, MLA_PAD)
    r = jnp.arange(LANES)[:, None]
    c = jnp.arange(MLA_PAD)[None, :]
    wr = ((r < MLA_ROPE) & (c % HEAD_PAD == MLA_NOPE + r)).astype(BF16)
    ones = ((c % HEAD_PAD) == jnp.where((c // HEAD_PAD) % 2 == 0, MLA_V, 0)).astype(F32)
    wqt = jnp.swapaxes(wq, 1, 2).astype(BF16)
    wvt = jnp.swapaxes(wv, 1, 2).astype(BF16)
    return wqt, wk.astype(BF16), wr, wvt, ones.reshape(MLA_PAD, 1)


def _block_diag(w):
    L, G, I, J = w.shape
    eye = jnp.eye(G, dtype=w.dtype)
    return jnp.einsum('lgij,gh->lgihj', w, eye).reshape(L, G * I, G * J)


def kernel(x, c, positions, mod_w, mod_b, norm1, w_in, mla_q_norm, mla_w_uq, mla_kv_norm, mla_w_ukv, ret_gn, lru_conv_w, lru_conv_b, lru_w_a, lru_b_a, lru_w_i, lru_b_i, lru_lambda, w_out, norm2, w_gate_up, w_down, final_norm, final_mod_w, final_mod_b):
    B, S, D = x.shape
    L = mod_w.shape[0]
    tm = min(S, 512)
    tt = tm // B
    tq = min(S, 256)
    ts = min(S, 128)
    row = lambda a: a.reshape(L, 1, a.shape[-1])

    mod = _modulation(c, mod_w, mod_b, 1536).reshape(L, B, N_MOD, D)
    fmod = _modulation(c, final_mod_w[None], final_mod_b[None], 1024).reshape(B, 2, D)
    fnorm = final_norm.reshape(1, D)
    tables = _rope_tables(positions)

    w_in_p = _pad_in_proj(w_in)
    wqt, wk, wr, wvt, ones = _mla_weights(mla_w_uq, mla_w_ukv)
    wa = _block_diag(lru_w_a).astype(BF16)
    wi = _block_diag(lru_w_i).astype(BF16)
    wo = w_out.astype(BF16)
    wgu = w_gate_up.astype(BF16)
    wd = w_down.astype(BF16)
    ret_consts = _retention_consts()

    for l in range(L):
        zm, zr, zl = _in_proj(x, mod, row(norm1), w_in_p, l, tt)
        qt, k, vt = _mla_prep(zm, tables, mla_q_norm, mla_kv_norm, wqt, wk, wr, wvt, ones,
                              l, tm, tq)
        ya = _attention(qt, k, vt, tq)
        yb = _retention(zr, tables[0], tables[1], ret_consts, row(ret_gn), l)
        yc = _rg_lru(zl, lru_conv_w, row(lru_conv_b), wa, row(lru_b_a), wi, row(lru_b_i),
                     row(lru_lambda), l, ts)
        x = _out_ffn(x, ya, yb, yc, mod, row(norm2), wo, wgu, wd, fmod, fnorm, l, tt,
                     final=(l == L - 1))
    return x
```

```python
import functools

import jax
import jax.numpy as jnp
from jax import lax
from jax.experimental import pallas as pl
from jax.experimental.pallas import tpu as pltpu

D_MODEL = 1024
MLA_HEADS = 6
MLA_Q_RANK = 256
MLA_KV_RANK = 128
MLA_NOPE = 64
MLA_ROPE = 32
MLA_V = 64
MLA_WIDTH = MLA_HEADS * MLA_V
RET_HEADS = 4
RET_DK = 32
RET_DV = 64
RET_WIDTH = RET_HEADS * RET_DV
RET_CHUNK = 128
RET_GROUP = 4
LRU_WIDTH = D_MODEL - MLA_WIDTH - RET_WIDTH
LRU_BLOCKS = 6
LRU_BLOCK = LRU_WIDTH // LRU_BLOCKS
CONV_WIDTH = 4
LRU_C = 8.0
FFN_HIDDEN = 2816
ROPE_BASE = 10000.0
NORM_EPS = 1e-6
LOG2_E = 1.4426950408889634
N_MOD = 6

LANES = 128
HEAD_PAD = LANES
MLA_PAD = MLA_HEADS * HEAD_PAD
Z_MLA = 512
Z_RET = 2 * RET_HEADS * RET_DK + 2 * RET_WIDTH
Z_LRU = 2 * LRU_WIDTH
Z_ALL = Z_MLA + Z_RET + Z_LRU
FFN_CHUNK = 256
VMEM_LIMIT = 56 * 1024 * 1024

BF16 = jnp.bfloat16
F32 = jnp.float32


def _params(*sem):
    return pltpu.CompilerParams(dimension_semantics=sem, vmem_limit_bytes=VMEM_LIMIT)


def _resident(shape, index_map):
    return pl.BlockSpec(shape, index_map, pipeline_mode=pl.Buffered(1))


def _rms(x, g):
    return x * lax.rsqrt(jnp.mean(x * x, axis=-1, keepdims=True) + NORM_EPS) * g


def _swap_halves(x, first_half):
    return jnp.where(first_half, pltpu.roll(x, LANES - 16, 1), pltpu.roll(x, 16, 1))


def _mod_kernel(c_ref, w_ref, b_ref, o_ref):
    c = c_ref[...]
    ca = (c * jax.nn.sigmoid(c)).astype(BF16)
    o_ref[...] = jnp.dot(ca, w_ref[...].astype(BF16), preferred_element_type=F32) + b_ref[...]


def _modulation(c, w, b, tn):
    L, D, N = w.shape
    B = c.shape[0]
    return pl.pallas_call(
        _mod_kernel,
        grid=(L, N // tn),
        in_specs=[pl.BlockSpec((B, D), lambda l, j: (0, 0)),
                  pl.BlockSpec((None, D, tn), lambda l, j: (l, 0, j)),
                  pl.BlockSpec((None, 1, tn), lambda l, j: (l, 0, j))],
        out_specs=pl.BlockSpec((None, B, tn), lambda l, j: (l, 0, j)),
        out_shape=jax.ShapeDtypeStruct((L, B, N), F32),
        compiler_params=_params("parallel", "parallel"),
        name="modulation",
    )(c, w, b.reshape(L, 1, N))


def _rope_table_kernel(pos_ref, posr_ref, inv_ref, invc_ref, cos_ref, sin_ref, cost_ref, sint_ref):
    ang = pos_ref[...].astype(F32) * inv_ref[...]
    lane = lax.broadcasted_iota(jnp.int32, ang.shape, 1)
    first_half = (lane % 32) < 16
    cos_ref[...] = jnp.cos(ang)
    sin = jnp.sin(ang)
    sin_ref[...] = jnp.where(first_half, -sin, sin)
    ang_t = invc_ref[...] * posr_ref[...].astype(F32)
    cost_ref[...] = jnp.cos(ang_t)
    sint_ref[...] = jnp.sin(ang_t)


def _rope_tables(positions):
    B, S = positions.shape
    half = MLA_ROPE // 2
    inv = ROPE_BASE ** (-jnp.arange(half, dtype=F32) / half)
    ts = min(S, 1024)
    spec = pl.BlockSpec((None, ts, LANES), lambda b, i: (b, i, 0))
    spec_t = pl.BlockSpec((None, half, ts), lambda b, i: (b, 0, i))
    return pl.pallas_call(
        _rope_table_kernel,
        grid=(B, S // ts),
        in_specs=[pl.BlockSpec((None, ts, 1), lambda b, i: (b, i, 0)),
                  pl.BlockSpec((None, 1, ts), lambda b, i: (b, 0, i)),
                  pl.BlockSpec((1, LANES), lambda b, i: (0, 0)),
                  pl.BlockSpec((half, 1), lambda b, i: (0, 0))],
        out_specs=[spec, spec, spec_t, spec_t],
        out_shape=[jax.ShapeDtypeStruct((B, S, LANES), F32)] * 2
        + [jax.ShapeDtypeStruct((B, half, S), F32)] * 2,
        compiler_params=_params("parallel", "parallel"),
        name="rope_tables",
    )(positions.reshape(B, S, 1), positions.reshape(B, 1, S),
      jnp.tile(inv, LANES // half).reshape(1, LANES), inv.reshape(half, 1))


def _softplus(x):
    return jnp.maximum(x, 0.0) + jnp.log1p(jnp.exp(-jnp.abs(x)))


def _gelu_tanh(x):
    return 0.5 * x * (1.0 + jnp.tanh(0.7978845608028654 * (x + 0.044715 * (x * x * x))))


def _sigmoid(x):
    return 0.5 * jnp.tanh(0.5 * x) + 0.5


def _in_proj_lru_kernel(x_ref, mod_ref, g_ref, w_ref, cw_ref, cb_ref, wa_ref, ba_ref, wi_ref,
                        bi_ref, lam_ref, zm_ref, zr_ref, yc_ref, xbuf, h_s):
    B, tt, D = x_ref.shape
    W = LRU_WIDTH
    PAD = 8
    taps = CONV_WIDTH - 1

    @pl.when(pl.program_id(0) == 0)
    def _():
        xbuf[0:PAD] = jnp.zeros((PAD, B, W), F32)
        h_s[...] = jnp.zeros((B, W), F32)

    mod = mod_ref[...]
    h = _rms(x_ref[...], g_ref[...]) * (1.0 + mod[:, 1:2]) + mod[:, 0:1]
    h = h.reshape(B * tt, D).astype(BF16)

    zl = jnp.dot(h, w_ref[:, Z_MLA + Z_RET:], preferred_element_type=F32)
    zm_ref[...] = jnp.dot(h, w_ref[:, :Z_MLA], preferred_element_type=F32).reshape(B, tt, Z_MLA)
    zl = pltpu.einshape("btd->tbd", zl.reshape(B, tt, Z_LRU))
    gate = _gelu_tanh(zl[:, :, W:])

    xbuf[PAD:PAD + tt] = zl[:, :, :W]
    cw = cw_ref[...]
    xc = cb_ref[...].reshape(1, 1, W) + sum(
        xbuf[PAD - taps + j:PAD - taps + j + tt] * cw[j].reshape(1, 1, W)
        for j in range(CONV_WIDTH))
    xbuf[PAD - taps:PAD] = xbuf[PAD + tt - taps:PAD + tt]

    xc2 = xc.reshape(tt * B, W)
    xb = xc2.astype(BF16)
    r = _sigmoid(jnp.dot(xb, wa_ref[...], preferred_element_type=F32) + ba_ref[...])
    i = _sigmoid(jnp.dot(xb, wi_ref[...], preferred_element_type=F32) + bi_ref[...])
    zr_ref[...] = jnp.dot(h, w_ref[:, Z_MLA:Z_MLA + Z_RET],
                          preferred_element_type=F32).reshape(B, tt, Z_RET)

    log_a = (-LRU_C * _softplus(-lam_ref[...])) * r
    a = jnp.exp(log_a)
    one_minus_a2 = -jnp.tanh(log_a) * (a * a + 1.0)
    root = jnp.where(one_minus_a2 > 0.0, one_minus_a2 * lax.rsqrt(one_minus_a2), 0.0)
    a = a.reshape(tt, B, W)
    b = (root * i * xc2).reshape(tt, B, W)

    hid = h_s[...]
    for t in range(tt):
        hid = a[t] * hid + b[t]
        yc_ref[t] = gate[t] * hid
    h_s[...] = hid


def _in_proj_lru(x, mod, norm, w, cw, cb, wa, ba, wi, bi, lam, l, tt):
    B, S, D = x.shape
    W = LRU_WIDTH
    tok = lambda width: pl.BlockSpec((B, tt, width), lambda i: (0, i, 0))
    vec = pl.BlockSpec((None, 1, W), lambda i: (l, 0, 0))
    mat = _resident((None, W, W), lambda i: (l, 0, 0))
    return pl.pallas_call(
        _in_proj_lru_kernel,
        grid=(S // tt,),
        in_specs=[tok(D),
                  pl.BlockSpec((None, B, N_MOD, D), lambda i: (l, 0, 0, 0)),
                  pl.BlockSpec((None, 1, D), lambda i: (l, 0, 0)),
                  _resident((None, D, Z_ALL), lambda i: (l, 0, 0)),
                  pl.BlockSpec((None, CONV_WIDTH, W), lambda i: (l, 0, 0)),
                  vec, mat, vec, mat, vec, vec],
        out_specs=[tok(Z_MLA), tok(Z_RET),
                   pl.BlockSpec((tt, B, W), lambda i: (i, 0, 0))],
        out_shape=[jax.ShapeDtypeStruct((B, S, Z_MLA), F32),
                   jax.ShapeDtypeStruct((B, S, Z_RET), F32),
                   jax.ShapeDtypeStruct((S, B, W), F32)],
        scratch_shapes=[pltpu.VMEM((tt + 8, B, W), F32), pltpu.VMEM((B, W), F32)],
        compiler_params=_params("arbitrary"),
        name="in_proj_lru",
    )(x, mod, norm, w, cw, cb, wa, ba, wi, bi, lam)


def _rms_feature_major(x, g):
    return x * lax.rsqrt(jnp.mean(x * x, axis=0, keepdims=True) + NORM_EPS) * g


def _mla_prep_kernel(z_ref, cos_ref, sin_ref, cost_ref, sint_ref, qn_ref, kvn_ref, kvnr_ref,
                     wqt_ref, wk_ref, wr_ref, wvt_ref, one_ref, qt_ref, k_ref, vt_ref, *, tq):
    z = z_ref[...]
    cos = cos_ref[...]
    sin = sin_ref[...]
    lane = lax.broadcasted_iota(jnp.int32, cos.shape, 1)
    first_half = (lane % 32) < 16

    c_kv = _rms(z[:, MLA_Q_RANK:MLA_Q_RANK + MLA_KV_RANK], kvnr_ref[...]).astype(BF16)
    kr = z[:, MLA_Q_RANK + MLA_KV_RANK:]
    kr = jnp.where(lane < MLA_ROPE, kr * cos + _swap_halves(kr, first_half) * sin, 0.0)
    k = (jnp.dot(c_kv, wk_ref[...], preferred_element_type=F32)
         + jnp.dot(kr.astype(BF16), wr_ref[...], preferred_element_type=F32))
    k_ref[...] = k.astype(BF16)

    zt = z[:, :MLA_Q_RANK + MLA_KV_RANK].T
    c_q = _rms_feature_major(zt[:MLA_Q_RANK], qn_ref[...]).astype(BF16)
    q = jnp.dot(wqt_ref[...], c_q, preferred_element_type=F32)
    qk_scale = (MLA_NOPE + MLA_ROPE) ** -0.5 * LOG2_E
    cos_t = cost_ref[...] * qk_scale
    sin_t = sint_ref[...] * qk_scale
    half = MLA_ROPE // 2
    rows = []
    for h in range(MLA_HEADS):
        r0 = h * HEAD_PAD + MLA_NOPE
        x1 = q[r0:r0 + half]
        x2 = q[r0 + half:r0 + MLA_ROPE]
        rows += [q[h * HEAD_PAD:r0] * qk_scale, x1 * cos_t - x2 * sin_t, x2 * cos_t + x1 * sin_t,
                 q[r0 + MLA_ROPE:(h + 1) * HEAD_PAD]]
    q = jnp.concatenate(rows, axis=0).astype(BF16)
    c_kv_t = _rms_feature_major(zt[MLA_Q_RANK:], kvn_ref[...]).astype(BF16)
    v = (jnp.dot(wvt_ref[...], c_kv_t, preferred_element_type=F32) + one_ref[...]).astype(BF16)
    for t in range(z.shape[0] // tq):
        qt_ref[t] = q[:, t * tq:(t + 1) * tq]
        vt_ref[t] = v[:, t * tq:(t + 1) * tq]


def _mla_prep(zm, tables, qn, kvn, wqt, wk, wr, wvt, ones, l, tm, tq):
    B, S, _ = zm.shape
    cos, sin, cos_t, sin_t = tables
    tok = lambda w: pl.BlockSpec((None, tm, w), lambda b, i: (b, i, 0))
    tab_t = pl.BlockSpec((None, MLA_ROPE // 2, tm), lambda b, i: (b, 0, i))
    col = lambda n: pl.BlockSpec((None, n, 1), lambda b, i: (l, 0, 0))
    tr_spec = pl.BlockSpec((None, tm // tq, MLA_PAD, tq), lambda b, i: (b, i, 0, 0))
    tr_shape = jax.ShapeDtypeStruct((B, S // tq, MLA_PAD, tq), BF16)
    L = qn.shape[0]
    return pl.pallas_call(
        functools.partial(_mla_prep_kernel, tq=tq),
        grid=(B, S // tm),
        in_specs=[tok(Z_MLA), tok(LANES), tok(LANES), tab_t, tab_t,
                  col(MLA_Q_RANK), col(MLA_KV_RANK),
                  pl.BlockSpec((None, 1, MLA_KV_RANK), lambda b, i: (l, 0, 0)),
                  _resident((None, MLA_PAD, MLA_Q_RANK), lambda b, i: (l, 0, 0)),
                  _resident((None, MLA_KV_RANK, MLA_PAD), lambda b, i: (l, 0, 0)),
                  _resident((LANES, MLA_PAD), lambda b, i: (0, 0)),
                  _resident((None, MLA_PAD, MLA_KV_RANK), lambda b, i: (l, 0, 0)),
                  pl.BlockSpec((MLA_PAD, 1), lambda b, i: (0, 0))],
        out_specs=[tr_spec, tok(MLA_PAD), tr_spec],
        out_shape=[tr_shape, jax.ShapeDtypeStruct((B, S, MLA_PAD), BF16), tr_shape],
        compiler_params=_params("parallel", "parallel"),
        name="mla_prep",
    )(zm, cos, sin, cos_t, sin_t, qn.reshape(L, -1, 1), kvn.reshape(L, -1, 1),
      kvn.reshape(L, 1, -1), wqt, wk, wr, wvt, ones)


def _attn_kernel(qt_ref, k_ref, vt_ref, o_ref, s_s, m_s, acc_s, *, tq, nq):
    groups = tq // 8
    key = lax.broadcasted_iota(jnp.int32, (tq, tq), 0)
    qry = lax.broadcasted_iota(jnp.int32, (tq, tq), 1)
    causal = key <= qry
    low = lax.broadcasted_iota(jnp.int32, (HEAD_PAD, tq), 0) < MLA_V
    hs = lambda h: slice(h * HEAD_PAD, (h + 1) * HEAD_PAD)

    def pairs(n, step):
        def body(t, carry):
            step(2 * t)
            step(2 * t + 1)
            return carry

        lax.fori_loop(0, n // 2, body, 0)

        @pl.when(n % 2 == 1)
        def _():
            step(n - 1)

    def query_block(i, carry):
        def scores(j, masked):
            rows = pl.ds(pl.multiple_of(j * tq, tq), tq)
            for h in range(MLA_HEADS):
                s = jnp.dot(k_ref[rows, hs(h)], qt_ref[i, hs(h), :],
                            preferred_element_type=F32)
                if masked:
                    s = jnp.where(causal, s, -jnp.inf)
                s_s[h, j] = s
                m_s[h] = jnp.maximum(m_s[h], jnp.max(s.reshape(groups, 8, tq), axis=0))

        def values(j):
            for h in range(MLA_HEADS):
                p = jnp.exp2(s_s[h, j].reshape(groups, 8, tq) - m_s[h][None])
                acc_s[h] += jnp.dot(vt_ref[j, hs(h), :], p.reshape(tq, tq).astype(BF16),
                                    preferred_element_type=F32)

        m_s[...] = jnp.full(m_s.shape, -jnp.inf, F32)
        pairs(i, functools.partial(scores, masked=False))
        scores(i, True)
        for h in range(MLA_HEADS):
            m_s[h] = jnp.broadcast_to(jnp.max(m_s[h], axis=0, keepdims=True), (8, tq))
        acc_s[...] = jnp.zeros(acc_s.shape, F32)
        pairs(i + 1, values)

        rows = pl.ds(pl.multiple_of(i * tq, tq), tq)
        for p in range(MLA_HEADS // 2):
            even = acc_s[2 * p]
            odd = acc_s[2 * p + 1]
            pair = jnp.where(low, even / even[MLA_V:MLA_V + 1], odd / odd[0:1])
            o_ref[rows, p * LANES:(p + 1) * LANES] = pair.T.astype(BF16)
        return carry

    lax.fori_loop(0, nq, query_block, 0)


def _attention(qt, k, vt, tq):
    B, S, _ = k.shape
    nq = S // tq
    tr_spec = pl.BlockSpec((None, nq, MLA_PAD, tq), lambda b: (b, 0, 0, 0))
    return pl.pallas_call(
        functools.partial(_attn_kernel, tq=tq, nq=nq),
        grid=(B,),
        in_specs=[tr_spec, pl.BlockSpec((None, S, MLA_PAD), lambda b: (b, 0, 0)), tr_spec],
        out_specs=pl.BlockSpec((None, S, MLA_WIDTH), lambda b: (b, 0, 0)),
        out_shape=jax.ShapeDtypeStruct((B, S, MLA_WIDTH), BF16),
        scratch_shapes=[pltpu.VMEM((MLA_HEADS, nq, tq, tq), F32),
                        pltpu.VMEM((MLA_HEADS, 8, tq), F32),
                        pltpu.VMEM((MLA_HEADS, HEAD_PAD, tq), F32)],
        compiler_params=_params("parallel"),
        name="mla_attention",
    )(qt, k, vt)


def _retention_consts():
    C, H = RET_CHUNK, RET_HEADS
    log_g = jnp.log(1.0 - jnp.exp2(-5.0 - jnp.arange(H, dtype=F32)))
    idx = jnp.arange(C, dtype=F32)
    diff = idx[:, None] - idx[None, :]
    decay = jnp.where(diff >= 0, jnp.exp(log_g[:, None, None] * jnp.maximum(diff, 0.0)), 0.0)
    q_decay = jnp.exp(log_g[:, None] * (idx + 1.0))
    k_decay = jnp.exp(log_g[:, None] * (C - 1.0 - idx))
    chunk_decay = jnp.exp(log_g * C)
    qd = jnp.repeat(q_decay.T, RET_DK, axis=1)
    kd = jnp.repeat(k_decay.T, RET_DK, axis=1)
    cd = jnp.broadcast_to(jnp.repeat(chunk_decay, RET_DK)[:, None], (H * RET_DK, RET_WIDTH))
    return decay, qd, kd, cd


def _retention_kernel(z_ref, cos_ref, sin_ref, dec_ref, qd_ref, kd_ref, cd_ref, gn_ref, o_ref,
                      *, n_chunks, group_size):
    C = RET_CHUNK
    lane = lax.broadcasted_iota(jnp.int32, (C, LANES), 1)
    first_half = (lane % 32) < 16
    low = lane < RET_DV
    head_of_lane = lane // RET_DK
    srow = lax.broadcasted_iota(jnp.int32, (LANES, RET_WIDTH), 0) // RET_DK
    scol = lax.broadcasted_iota(jnp.int32, (LANES, RET_WIDTH), 1) // RET_DV
    same_head = srow == scol
    gn = gn_ref[...]
    cd = cd_ref[...]

    def rotate(x, rows):
        return x * cos_ref[rows, :] + _swap_halves(x, first_half) * sin_ref[rows, :]

    def group(n, state):
        rows = [pl.ds(pl.multiple_of((n * group_size + c) * C, C), C) for c in range(group_size)]
        q = [rotate(z_ref[r, 0:LANES], r) for r in rows]
        k = [rotate(z_ref[r, LANES:2 * LANES], r) * (RET_DK ** -0.5) for r in rows]
        kb = [x.astype(BF16) for x in k]
        v = [z_ref[r, 2 * LANES:2 * LANES + RET_WIDTH].astype(BF16) for r in rows]

        s = [[(lax.dot_general(jnp.where(head_of_lane == h, q[c], 0.0).astype(BF16), kb[c],
                               (((1,), (1,)), ((), ())), preferred_element_type=F32)
               * dec_ref[h]).astype(BF16) for h in range(RET_HEADS)] for c in range(group_size)]
        inner = [[jnp.where(low,
                            jnp.dot(s[c][2 * p], v[c][:, p * LANES:(p + 1) * LANES],
                                    preferred_element_type=F32),
                            jnp.dot(s[c][2 * p + 1], v[c][:, p * LANES:(p + 1) * LANES],
                                    preferred_element_type=F32))
                  for p in range(RET_HEADS // 2)] for c in range(group_size)]
        kv = [lax.dot_general((k[c] * kd_ref[...]).astype(BF16), v[c], (((0,), (0,)), ((), ())),
                              preferred_element_type=F32) for c in range(group_size)]
        states = [state]
        for c in range(group_size):
            states.append(states[c] * cd + jnp.where(same_head, kv[c], 0.0))
        cross = [jnp.dot((q[c] * qd_ref[...]).astype(BF16), states[c].astype(BF16),
                         preferred_element_type=F32) for c in range(group_size)]

        for c in range(group_size):
            for p in range(RET_HEADS // 2):
                cols = slice(p * LANES, (p + 1) * LANES)
                o = inner[c][p] + cross[c][:, cols]
                s_lo = jnp.sum(jnp.where(low, o, 0.0), axis=-1, keepdims=True)
                s_hi = jnp.sum(jnp.where(low, 0.0, o), axis=-1, keepdims=True)
                d = o - jnp.where(low, s_lo, s_hi) * (1.0 / RET_DV)
                d2 = d * d
                v_lo = jnp.sum(jnp.where(low, d2, 0.0), axis=-1, keepdims=True)
                v_hi = jnp.sum(jnp.where(low, 0.0, d2), axis=-1, keepdims=True)
                var = jnp.where(low, v_lo, v_hi) * (1.0 / RET_DV)
                y = d * lax.rsqrt(var + NORM_EPS) * gn[:, cols]
                gp = z_ref[rows[c], 2 * LANES + RET_WIDTH + p * LANES:
                           2 * LANES + RET_WIDTH + (p + 1) * LANES]
                o_ref[rows[c], cols] = (gp * jax.nn.sigmoid(gp) * y).astype(BF16)
        return states[group_size]

    lax.fori_loop(0, n_chunks // group_size, group, jnp.zeros((LANES, RET_WIDTH), F32))


def _retention(zr, cos, sin, consts, gn, l):
    B, S, _ = zr.shape
    decay, qd, kd, cd = consts
    full = lambda a: pl.BlockSpec(a.shape, lambda b: (0,) * a.ndim)
    return pl.pallas_call(
        functools.partial(_retention_kernel, n_chunks=S // RET_CHUNK,
                          group_size=RET_GROUP),
        grid=(B,),
        in_specs=[pl.BlockSpec((None, S, Z_RET), lambda b: (b, 0, 0)),
                  pl.BlockSpec((None, S, LANES), lambda b: (b, 0, 0)),
                  pl.BlockSpec((None, S, LANES), lambda b: (b, 0, 0)),
                  full(decay), full(qd), full(kd), full(cd),
                  pl.BlockSpec((None, 1, RET_WIDTH), lambda b: (l, 0, 0))],
        out_specs=pl.BlockSpec((None, S, RET_WIDTH), lambda b: (b, 0, 0)),
        out_shape=jax.ShapeDtypeStruct((B, S, RET_WIDTH), BF16),
        compiler_params=_params("parallel"),
        name="retention",
    )(zr, cos, sin, decay, qd, kd, cd, gn)


def _out_ffn_kernel(x_ref, ya_ref, yb_ref, yc_ref, mod_ref, g_ref, wo_ref, wg_ref, wu_ref,
                    wd_ref, fmod_ref, fg_ref, o_ref, h_s, acc_s, *, final):
    B, tt, D = x_ref.shape
    rows = B * tt
    mod = mod_ref[...]
    a0, b0 = MLA_WIDTH, MLA_WIDTH + RET_WIDTH
    yc = pltpu.einshape("tbd->btd", yc_ref[...]).reshape(rows, LRU_WIDTH).astype(BF16)
    y = (jnp.dot(ya_ref[...].reshape(rows, MLA_WIDTH), wo_ref[0:a0, :],
                 preferred_element_type=F32)
         + jnp.dot(yb_ref[...].reshape(rows, RET_WIDTH), wo_ref[a0:b0, :],
                   preferred_element_type=F32)
         + jnp.dot(yc, wo_ref[b0:, :], preferred_element_type=F32))
    x1 = x_ref[...] + mod[:, 2:3] * y.reshape(B, tt, D)
    o_ref[...] = x1
    h = _rms(x1, g_ref[...]) * (1.0 + mod[:, 4:5]) + mod[:, 3:4]
    h_s[...] = h.reshape(rows, D).astype(BF16)

    def gate_up(j):
        cols = slice(j * FFN_CHUNK, (j + 1) * FFN_CHUNK)
        h = h_s[...]
        return (jnp.dot(h, wg_ref[:, cols], preferred_element_type=F32),
                jnp.dot(h, wu_ref[:, cols], preferred_element_type=F32))

    n_chunks = FFN_HIDDEN // FFN_CHUNK
    nxt = gate_up(0)
    for j in range(n_chunks):
        gate, up = nxt
        if j + 1 < n_chunks:
            nxt = gate_up(j + 1)
        act = (gate * jax.nn.sigmoid(gate) * up).astype(BF16)
        down = jnp.dot(act, wd_ref[j * FFN_CHUNK:(j + 1) * FFN_CHUNK, :],
                       preferred_element_type=F32)
        acc_s[...] = down if j == 0 else acc_s[...] + down
    x2 = o_ref[...] + mod[:, 5:6] * acc_s[...].reshape(B, tt, D)
    if final:
        fmod = fmod_ref[...]
        x2 = _rms(x2, fg_ref[...]) * (1.0 + fmod[:, 1:2]) + fmod[:, 0:1]
    o_ref[...] = x2


def _out_ffn(x, ya, yb, yc, mod, norm, wo, wgu, wd, fmod, fnorm, l, tt, final):
    B, S, D = x.shape
    tok = lambda width: pl.BlockSpec((B, tt, width), lambda i: (0, i, 0))
    return pl.pallas_call(
        functools.partial(_out_ffn_kernel, final=final),
        grid=(S // tt,),
        in_specs=[tok(D), tok(MLA_WIDTH), tok(RET_WIDTH),
                  pl.BlockSpec((tt, B, LRU_WIDTH), lambda i: (i, 0, 0)),
                  pl.BlockSpec((None, B, N_MOD, D), lambda i: (l, 0, 0, 0)),
                  pl.BlockSpec((None, 1, D), lambda i: (l, 0, 0)),
                  _resident((None, D, D), lambda i: (l, 0, 0)),
                  _resident((None, D, FFN_HIDDEN), lambda i: (l, 0, 0)),
                  _resident((None, D, FFN_HIDDEN), lambda i: (l, 0, 1)),
                  _resident((None, FFN_HIDDEN, D), lambda i: (l, 0, 0)),
                  pl.BlockSpec((B, 2, D), lambda i: (0, 0, 0)),
                  pl.BlockSpec((1, D), lambda i: (0, 0))],
        out_specs=tok(D),
        out_shape=jax.ShapeDtypeStruct((B, S, D), F32),
        scratch_shapes=[pltpu.VMEM((B * tt, D), BF16), pltpu.VMEM((B * tt, D), F32)],
        compiler_params=_params("parallel"),
        name="out_ffn",
    )(x, ya, yb, yc, mod, norm, wo, wgu, wgu, wd, fmod, fnorm)


def _pad_in_proj(w_in):
    n_mla = MLA_Q_RANK + MLA_KV_RANK + MLA_ROPE
    pad = jnp.zeros(w_in.shape[:2] + (Z_MLA - n_mla,), w_in.dtype)
    return jnp.concatenate([w_in[..., :n_mla], pad, w_in[..., n_mla:]], axis=-1).astype(BF16)


def _pad_heads(w, width):
    L, R, H, _ = w.shape
    return jnp.pad(w, ((0, 0), (0, 0), (0, 0), (0, HEAD_PAD - width))).reshape(L, R, H * HEAD_PAD)


def _mla_weights(w_uq, w_ukv):
    L = w_uq.shape[0]
    wq = _pad_heads(w_uq.reshape(L, MLA_Q_RANK, MLA_HEADS, MLA_NOPE + MLA_ROPE), MLA_NOPE + MLA_ROPE)
    kv = w_ukv.reshape(L, MLA_KV_RANK, MLA_HEADS, MLA_NOPE + MLA_V)
    wk = _pad_heads(kv[..., :MLA_NOPE], MLA_NOPE)
    v = kv[..., MLA_NOPE:]
    odd = (jnp.arange(MLA_HEADS) % 2 == 1)[None, None, :, None]
    zero = jnp.zeros_like(v)
    wv = jnp.concatenate([jnp.where(odd, zero, v), jnp.where(odd, v, zero)], axis=-1)
    wv = wv.reshape(L, MLA_KV_RANK, MLA_PAD)
    r = jnp.arange(LANES)[:, None]
    c = jnp.arange(MLA_PAD)[None, :]
    wr = ((r < MLA_ROPE) & (c % HEAD_PAD == MLA_NOPE + r)).astype(BF16)
    ones = ((c % HEAD_PAD) == jnp.where((c // HEAD_PAD) % 2 == 0, MLA_V, 0)).astype(F32)
    wqt = jnp.swapaxes(wq, 1, 2).astype(BF16)
    wvt = jnp.swapaxes(wv, 1, 2).astype(BF16)
    return wqt, wk.astype(BF16), wr, wvt, ones.reshape(MLA_PAD, 1)


def _block_diag(w):
    L, G, I, J = w.shape
    eye = jnp.eye(G, dtype=w.dtype)
    return jnp.einsum('lgij,gh->lgihj', w, eye).reshape(L, G * I, G * J)


def kernel(x, c, positions, mod_w, mod_b, norm1, w_in, mla_q_norm, mla_w_uq, mla_kv_norm, mla_w_ukv, ret_gn, lru_conv_w, lru_conv_b, lru_w_a, lru_b_a, lru_w_i, lru_b_i, lru_lambda, w_out, norm2, w_gate_up, w_down, final_norm, final_mod_w, final_mod_b):
    B, S, D = x.shape
    L = mod_w.shape[0]
    tm = min(S, 512)
    tt = tm // B
    tq = min(S, 256)
    row = lambda a: a.reshape(L, 1, a.shape[-1])

    mod = _modulation(c, mod_w, mod_b, 1536).reshape(L, B, N_MOD, D)
    fmod = _modulation(c, final_mod_w[None], final_mod_b[None], 1024).reshape(B, 2, D)
    fnorm = final_norm.reshape(1, D)
    tables = _rope_tables(positions)

    w_in_p = _pad_in_proj(w_in)
    wqt, wk, wr, wvt, ones = _mla_weights(mla_w_uq, mla_w_ukv)
    wa = _block_diag(lru_w_a).astype(BF16)
    wi = _block_diag(lru_w_i).astype(BF16)
    wo = w_out.astype(BF16)
    wgu = w_gate_up.astype(BF16)
    wd = w_down.astype(BF16)
    ret_consts = _retention_consts()

    for l in range(L):
        zm, zr, yc = _in_proj_lru(x, mod, row(norm1), w_in_p, lru_conv_w, row(lru_conv_b), wa,
                                  row(lru_b_a), wi, row(lru_b_i), row(lru_lambda), l, tt)
        qt, k, vt = _mla_prep(zm, tables, mla_q_norm, mla_kv_norm, wqt, wk, wr, wvt, ones,
                              l, tm, tq)
        ya = _attention(qt, k, vt, tq)
        yb = _retention(zr, tables[0], tables[1], ret_consts, row(ret_gn), l)
        x = _out_ffn(x, ya, yb, yc, mod, row(norm2), wo, wgu, wd, fmod, fnorm, l, tt,
                     final=(l == L - 1))
    return x
```

```python
import functools

import jax
import jax.numpy as jnp
from jax import lax
from jax.experimental import pallas as pl
from jax.experimental.pallas import tpu as pltpu

D_MODEL = 1024
MLA_HEADS = 6
MLA_Q_RANK = 256
MLA_KV_RANK = 128
MLA_NOPE = 64
MLA_ROPE = 32
MLA_V = 64
MLA_WIDTH = MLA_HEADS * MLA_V
RET_HEADS = 4
RET_DK = 32
RET_DV = 64
RET_WIDTH = RET_HEADS * RET_DV
RET_CHUNK = 128
RET_GROUP = 4
LRU_WIDTH = D_MODEL - MLA_WIDTH - RET_WIDTH
LRU_BLOCKS = 6
LRU_BLOCK = LRU_WIDTH // LRU_BLOCKS
CONV_WIDTH = 4
LRU_C = 8.0
FFN_HIDDEN = 2816
ROPE_BASE = 10000.0
NORM_EPS = 1e-6
LOG2_E = 1.4426950408889634
N_MOD = 6

LANES = 128
HEAD_PAD = LANES
MLA_PAD = MLA_HEADS * HEAD_PAD
Z_MLA = 512
Z_RET = 2 * RET_HEADS * RET_DK + 2 * RET_WIDTH
Z_LRU = 2 * LRU_WIDTH
FFN_CHUNK = 256
VMEM_LIMIT = 56 * 1024 * 1024

BF16 = jnp.bfloat16
F32 = jnp.float32


def _params(*sem):
    return pltpu.CompilerParams(dimension_semantics=sem, vmem_limit_bytes=VMEM_LIMIT)


def _resident(shape, index_map):
    return pl.BlockSpec(shape, index_map, pipeline_mode=pl.Buffered(1))


def _rms(x, g):
    return x * lax.rsqrt(jnp.mean(x * x, axis=-1, keepdims=True) + NORM_EPS) * g


def _swap_halves(x, first_half):
    return jnp.where(first_half, pltpu.roll(x, LANES - 16, 1), pltpu.roll(x, 16, 1))


def _mod_kernel(c_ref, w_ref, b_ref, o_ref):
    c = c_ref[...]
    ca = (c * jax.nn.sigmoid(c)).astype(BF16)
    o_ref[...] = jnp.dot(ca, w_ref[...].astype(BF16), preferred_element_type=F32) + b_ref[...]


def _modulation(c, w, b, tn):
    L, D, N = w.shape
    B = c.shape[0]
    return pl.pallas_call(
        _mod_kernel,
        grid=(L, N // tn),
        in_specs=[pl.BlockSpec((B, D), lambda l, j: (0, 0)),
                  pl.BlockSpec((None, D, tn), lambda l, j: (l, 0, j)),
                  pl.BlockSpec((None, 1, tn), lambda l, j: (l, 0, j))],
        out_specs=pl.BlockSpec((None, B, tn), lambda l, j: (l, 0, j)),
        out_shape=jax.ShapeDtypeStruct((L, B, N), F32),
        compiler_params=_params("parallel", "parallel"),
        name="modulation",
    )(c, w, b.reshape(L, 1, N))


def _rope_table_kernel(pos_ref, inv_ref, cos_ref, sin_ref, cost_ref, sint_ref):
    ang = inv_ref[...] * pos_ref[...].astype(F32)
    cos = jnp.cos(ang)
    sin = jnp.sin(ang)
    cost_ref[...] = cos
    sint_ref[...] = sin
    reps = LANES // (2 * cos.shape[0])
    cos_ref[...] = jnp.concatenate([cos] * (2 * reps), axis=0).T
    sin_ref[...] = jnp.concatenate([-sin, sin] * reps, axis=0).T


def _rope_tables(positions):
    B, S = positions.shape
    half = MLA_ROPE // 2
    inv = ROPE_BASE ** (-jnp.arange(half, dtype=F32) / half)
    ts = min(S, 1024)
    spec = pl.BlockSpec((None, ts, LANES), lambda b, i: (b, i, 0))
    spec_t = pl.BlockSpec((None, half, ts), lambda b, i: (b, 0, i))
    return pl.pallas_call(
        _rope_table_kernel,
        grid=(B, S // ts),
        in_specs=[pl.BlockSpec((None, 1, ts), lambda b, i: (b, 0, i)),
                  pl.BlockSpec((half, 1), lambda b, i: (0, 0))],
        out_specs=[spec, spec, spec_t, spec_t],
        out_shape=[jax.ShapeDtypeStruct((B, S, LANES), F32)] * 2
        + [jax.ShapeDtypeStruct((B, half, S), F32)] * 2,
        compiler_params=_params("parallel", "parallel"),
        name="rope_tables",
    )(positions.reshape(B, 1, S), inv.reshape(half, 1))


def _softplus(x):
    return jnp.maximum(x, 0.0) + jnp.log1p(jnp.exp(-jnp.abs(x)))


def _gelu_tanh(x):
    return 0.5 * x * (1.0 + jnp.tanh(0.7978845608028654 * (x + 0.044715 * (x * x * x))))


def _sigmoid(x):
    return 0.5 * jnp.tanh(0.5 * x) + 0.5


def _in_proj_lru_kernel(x_ref, mod_ref, g_ref, wm_ref, wr_ref, wl_ref, cw_ref, cb_ref, wa_ref,
                        ba_ref, wi_ref, bi_ref, lam_ref, zm_ref, zr_ref, yc_ref, xbuf, h_s):
    B, tt, D = x_ref.shape
    W = LRU_WIDTH
    PAD = 8
    taps = CONV_WIDTH - 1

    @pl.when(pl.program_id(0) == 0)
    def _():
        xbuf[0:PAD] = jnp.zeros((PAD, B, W), F32)
        h_s[...] = jnp.zeros((B, W), F32)

    mod = mod_ref[...]
    h = _rms(x_ref[...], g_ref[...]) * (1.0 + mod[:, 1:2]) + mod[:, 0:1]
    h = h.reshape(B * tt, D).astype(BF16)

    zl = jnp.dot(h, wl_ref[...], preferred_element_type=F32)
    zm_ref[...] = jnp.dot(h, wm_ref[...], preferred_element_type=F32).reshape(B, tt, Z_MLA)
    zl = pltpu.einshape("btd->tbd", zl.reshape(B, tt, Z_LRU))
    gate = _gelu_tanh(zl[:, :, W:])

    xbuf[PAD:PAD + tt] = zl[:, :, :W]
    cw = cw_ref[...]
    xc = cb_ref[...].reshape(1, 1, W) + sum(
        xbuf[PAD - taps + j:PAD - taps + j + tt] * cw[j].reshape(1, 1, W)
        for j in range(CONV_WIDTH))
    xbuf[PAD - taps:PAD] = xbuf[PAD + tt - taps:PAD + tt]

    xc2 = xc.reshape(tt * B, W)
    xb = xc2.astype(BF16)
    r = _sigmoid(jnp.dot(xb, wa_ref[...], preferred_element_type=F32) + ba_ref[...])
    i = _sigmoid(jnp.dot(xb, wi_ref[...], preferred_element_type=F32) + bi_ref[...])
    zr_ref[...] = jnp.dot(h, wr_ref[...], preferred_element_type=F32).reshape(B, tt, Z_RET)

    log_a = (-LRU_C * _softplus(-lam_ref[...])) * r
    a = jnp.exp(log_a)
    one_minus_a2 = -jnp.tanh(log_a) * (a * a + 1.0)
    root = jnp.where(one_minus_a2 > 0.0, one_minus_a2 * lax.rsqrt(one_minus_a2), 0.0)
    a = a.reshape(tt, B, W)
    b = (root * i * xc2).reshape(tt, B, W)

    hid = h_s[...]
    for t in range(tt):
        hid = a[t] * hid + b[t]
        yc_ref[t] = gate[t] * hid
    h_s[...] = hid


def _in_proj_lru(x, mod, norm, w_groups, cw, cb, wa, ba, wi, bi, lam, l, tt):
    B, S, D = x.shape
    W = LRU_WIDTH
    tok = lambda width: pl.BlockSpec((B, tt, width), lambda i: (0, i, 0))
    vec = pl.BlockSpec((None, 1, W), lambda i: (l, 0, 0))
    mat = _resident((None, W, W), lambda i: (l, 0, 0))
    return pl.pallas_call(
        _in_proj_lru_kernel,
        grid=(S // tt,),
        in_specs=[tok(D),
                  pl.BlockSpec((None, B, N_MOD, D), lambda i: (l, 0, 0, 0)),
                  pl.BlockSpec((None, 1, D), lambda i: (l, 0, 0)),
                  *[_resident((None, D, w.shape[-1]), lambda i: (l, 0, 0)) for w in w_groups],
                  pl.BlockSpec((None, CONV_WIDTH, W), lambda i: (l, 0, 0)),
                  vec, mat, vec, mat, vec, vec],
        out_specs=[tok(Z_MLA), tok(Z_RET),
                   pl.BlockSpec((tt, B, W), lambda i: (i, 0, 0))],
        out_shape=[jax.ShapeDtypeStruct((B, S, Z_MLA), F32),
                   jax.ShapeDtypeStruct((B, S, Z_RET), F32),
                   jax.ShapeDtypeStruct((S, B, W), F32)],
        scratch_shapes=[pltpu.VMEM((tt + 8, B, W), F32), pltpu.VMEM((B, W), F32)],
        compiler_params=_params("arbitrary"),
        name="in_proj_lru",
    )(x, mod, norm, *w_groups, cw, cb, wa, ba, wi, bi, lam)


def _rms_feature_major(x, g):
    return x * lax.rsqrt(jnp.mean(x * x, axis=0, keepdims=True) + NORM_EPS) * g


def _mla_prep_kernel(z_ref, cos_ref, sin_ref, cost_ref, sint_ref, qn_ref, kvn_ref, kvnr_ref,
                     wqt_ref, wk_ref, wr_ref, wvt_ref, one_ref, qt_ref, k_ref, vt_ref, *, tq):
    z = z_ref[...]
    cos = cos_ref[...]
    sin = sin_ref[...]
    lane = lax.broadcasted_iota(jnp.int32, cos.shape, 1)
    first_half = (lane % 32) < 16

    c_kv = _rms(z[:, MLA_Q_RANK:MLA_Q_RANK + MLA_KV_RANK], kvnr_ref[...]).astype(BF16)
    kr = z[:, MLA_Q_RANK + MLA_KV_RANK:]
    kr = jnp.where(lane < MLA_ROPE, kr * cos + _swap_halves(kr, first_half) * sin, 0.0)
    k = (jnp.dot(c_kv, wk_ref[...], preferred_element_type=F32)
         + jnp.dot(kr.astype(BF16), wr_ref[...], preferred_element_type=F32))
    k_ref[...] = k.astype(BF16)

    zt = z[:, :MLA_Q_RANK + MLA_KV_RANK].T
    c_q = _rms_feature_major(zt[:MLA_Q_RANK], qn_ref[...]).astype(BF16)
    q = jnp.dot(wqt_ref[...], c_q, preferred_element_type=F32)
    qk_scale = (MLA_NOPE + MLA_ROPE) ** -0.5 * LOG2_E
    cos_t = cost_ref[...] * qk_scale
    sin_t = sint_ref[...] * qk_scale
    half = MLA_ROPE // 2
    rows = []
    for h in range(MLA_HEADS):
        r0 = h * HEAD_PAD + MLA_NOPE
        x1 = q[r0:r0 + half]
        x2 = q[r0 + half:r0 + MLA_ROPE]
        rows += [q[h * HEAD_PAD:r0] * qk_scale, x1 * cos_t - x2 * sin_t, x2 * cos_t + x1 * sin_t,
                 q[r0 + MLA_ROPE:(h + 1) * HEAD_PAD]]
    q = jnp.concatenate(rows, axis=0).astype(BF16)
    c_kv_t = _rms_feature_major(zt[MLA_Q_RANK:], kvn_ref[...]).astype(BF16)
    v = (jnp.dot(wvt_ref[...], c_kv_t, preferred_element_type=F32) + one_ref[...]).astype(BF16)
    for t in range(z.shape[0] // tq):
        qt_ref[t] = q[:, t * tq:(t + 1) * tq]
        vt_ref[t] = v[:, t * tq:(t + 1) * tq]


def _mla_prep(zm, tables, qn, kvn, wqt, wk, wr, wvt, ones, l, tm, tq):
    B, S, _ = zm.shape
    cos, sin, cos_t, sin_t = tables
    tok = lambda w: pl.BlockSpec((None, tm, w), lambda b, i: (b, i, 0))
    tab_t = pl.BlockSpec((None, MLA_ROPE // 2, tm), lambda b, i: (b, 0, i))
    col = lambda n: pl.BlockSpec((None, n, 1), lambda b, i: (l, 0, 0))
    tr_spec = pl.BlockSpec((None, tm // tq, MLA_PAD, tq), lambda b, i: (b, i, 0, 0))
    tr_shape = jax.ShapeDtypeStruct((B, S // tq, MLA_PAD, tq), BF16)
    L = qn.shape[0]
    return pl.pallas_call(
        functools.partial(_mla_prep_kernel, tq=tq),
        grid=(B, S // tm),
        in_specs=[tok(Z_MLA), tok(LANES), tok(LANES), tab_t, tab_t,
                  col(MLA_Q_RANK), col(MLA_KV_RANK),
                  pl.BlockSpec((None, 1, MLA_KV_RANK), lambda b, i: (l, 0, 0)),
                  _resident((None, MLA_PAD, MLA_Q_RANK), lambda b, i: (l, 0, 0)),
                  _resident((None, MLA_KV_RANK, MLA_PAD), lambda b, i: (l, 0, 0)),
                  _resident((LANES, MLA_PAD), lambda b, i: (0, 0)),
                  _resident((None, MLA_PAD, MLA_KV_RANK), lambda b, i: (l, 0, 0)),
                  pl.BlockSpec((MLA_PAD, 1), lambda b, i: (0, 0))],
        out_specs=[tr_spec, tok(MLA_PAD), tr_spec],
        out_shape=[tr_shape, jax.ShapeDtypeStruct((B, S, MLA_PAD), BF16), tr_shape],
        compiler_params=_params("parallel", "parallel"),
        name="mla_prep",
    )(zm, cos, sin, cos_t, sin_t, qn.reshape(L, -1, 1), kvn.reshape(L, -1, 1),
      kvn.reshape(L, 1, -1), wqt, wk, wr, wvt, ones)


def _attn_kernel(qt_ref, k_ref, vt_ref, o_ref, s_s, m_s, acc_s, *, tq, nq):
    groups = tq // 8
    key = lax.broadcasted_iota(jnp.int32, (tq, tq), 0)
    qry = lax.broadcasted_iota(jnp.int32, (tq, tq), 1)
    causal = key <= qry
    low = lax.broadcasted_iota(jnp.int32, (HEAD_PAD, tq), 0) < MLA_V
    hs = lambda h: slice(h * HEAD_PAD, (h + 1) * HEAD_PAD)

    def query_block(i):
        for j in range(i + 1):
            for h in range(MLA_HEADS):
                s = jnp.dot(k_ref[j * tq:(j + 1) * tq, hs(h)], qt_ref[i, hs(h), :],
                            preferred_element_type=F32)
                if j == i:
                    s = jnp.where(causal, s, -jnp.inf)
                s_s[h, j] = s
                m = jnp.max(s.reshape(groups, 8, tq), axis=0)
                m_s[h] = m if j == 0 else jnp.maximum(m_s[h], m)
        for h in range(MLA_HEADS):
            m_s[h] = jnp.broadcast_to(jnp.max(m_s[h], axis=0, keepdims=True), (8, tq))
        for j in range(i + 1):
            for h in range(MLA_HEADS):
                p = jnp.exp2(s_s[h, j].reshape(groups, 8, tq) - m_s[h][None])
                pv = jnp.dot(vt_ref[j, hs(h), :], p.reshape(tq, tq).astype(BF16),
                             preferred_element_type=F32)
                acc_s[h] = pv if j == 0 else acc_s[h] + pv

        for p in range(MLA_HEADS // 2):
            even = acc_s[2 * p]
            odd = acc_s[2 * p + 1]
            pair = jnp.where(low, even / even[MLA_V:MLA_V + 1], odd / odd[0:1])
            o_ref[i * tq:(i + 1) * tq, p * LANES:(p + 1) * LANES] = pair.T.astype(BF16)

    def body(i, carry):
        lax.switch(i, [functools.partial(query_block, n) for n in range(nq)])
        return carry

    lax.fori_loop(0, nq, body, 0)


def _attention(qt, k, vt, tq):
    B, S, _ = k.shape
    nq = S // tq
    tr_spec = pl.BlockSpec((None, nq, MLA_PAD, tq), lambda b: (b, 0, 0, 0))
    return pl.pallas_call(
        functools.partial(_attn_kernel, tq=tq, nq=nq),
        grid=(B,),
        in_specs=[tr_spec, pl.BlockSpec((None, S, MLA_PAD), lambda b: (b, 0, 0)), tr_spec],
        out_specs=pl.BlockSpec((None, S, MLA_WIDTH), lambda b: (b, 0, 0)),
        out_shape=jax.ShapeDtypeStruct((B, S, MLA_WIDTH), BF16),
        scratch_shapes=[pltpu.VMEM((MLA_HEADS, nq, tq, tq), F32),
                        pltpu.VMEM((MLA_HEADS, 8, tq), F32),
                        pltpu.VMEM((MLA_HEADS, HEAD_PAD, tq), F32)],
        compiler_params=_params("parallel"),
        name="mla_attention",
    )(qt, k, vt)


def _retention_consts():
    C, H = RET_CHUNK, RET_HEADS
    log_g = jnp.log(1.0 - jnp.exp2(-5.0 - jnp.arange(H, dtype=F32)))
    idx = jnp.arange(C, dtype=F32)
    diff = idx[:, None] - idx[None, :]
    decay = jnp.where(diff >= 0, jnp.exp(log_g[:, None, None] * jnp.maximum(diff, 0.0)), 0.0)
    q_decay = jnp.exp(log_g[:, None] * (idx + 1.0))
    k_decay = jnp.exp(log_g[:, None] * (C - 1.0 - idx))
    chunk_decay = jnp.exp(log_g * C)
    qd = jnp.repeat(q_decay.T, RET_DK, axis=1)
    kd = jnp.repeat(k_decay.T, RET_DK, axis=1)
    cd = jnp.broadcast_to(jnp.repeat(chunk_decay, RET_DK)[:, None], (H * RET_DK, RET_WIDTH))
    return decay, qd, kd, cd


def _retention_kernel(z_ref, cos_ref, sin_ref, dec_ref, qd_ref, kd_ref, cd_ref, gn_ref, o_ref,
                      *, n_chunks, group_size):
    C = RET_CHUNK
    lane = lax.broadcasted_iota(jnp.int32, (C, LANES), 1)
    first_half = (lane % 32) < 16
    low = lane < RET_DV
    head_of_lane = lane // RET_DK
    srow = lax.broadcasted_iota(jnp.int32, (LANES, RET_WIDTH), 0) // RET_DK
    scol = lax.broadcasted_iota(jnp.int32, (LANES, RET_WIDTH), 1) // RET_DV
    same_head = srow == scol
    gn = gn_ref[...]
    cd = cd_ref[...]

    def rotate(x, rows):
        return x * cos_ref[rows, :] + _swap_halves(x, first_half) * sin_ref[rows, :]

    def group(n, state):
        rows = [pl.ds(pl.multiple_of((n * group_size + c) * C, C), C) for c in range(group_size)]
        q = [rotate(z_ref[r, 0:LANES], r) for r in rows]
        k = [rotate(z_ref[r, LANES:2 * LANES], r) * (RET_DK ** -0.5) for r in rows]
        kb = [x.astype(BF16) for x in k]
        v = [z_ref[r, 2 * LANES:2 * LANES + RET_WIDTH].astype(BF16) for r in rows]

        s = [[(lax.dot_general(jnp.where(head_of_lane == h, q[c], 0.0).astype(BF16), kb[c],
                               (((1,), (1,)), ((), ())), preferred_element_type=F32)
               * dec_ref[h]).astype(BF16) for h in range(RET_HEADS)] for c in range(group_size)]
        inner = [[jnp.where(low,
                            jnp.dot(s[c][2 * p], v[c][:, p * LANES:(p + 1) * LANES],
                                    preferred_element_type=F32),
                            jnp.dot(s[c][2 * p + 1], v[c][:, p * LANES:(p + 1) * LANES],
                                    preferred_element_type=F32))
                  for p in range(RET_HEADS // 2)] for c in range(group_size)]
        kv = [lax.dot_general((k[c] * kd_ref[...]).astype(BF16), v[c], (((0,), (0,)), ((), ())),
                              preferred_element_type=F32) for c in range(group_size)]
        states = [state]
        for c in range(group_size):
            states.append(states[c] * cd + jnp.where(same_head, kv[c], 0.0))
        cross = [jnp.dot((q[c] * qd_ref[...]).astype(BF16), states[c].astype(BF16),
                         preferred_element_type=F32) for c in range(group_size)]

        for c in range(group_size):
            for p in range(RET_HEADS // 2):
                cols = slice(p * LANES, (p + 1) * LANES)
                o = inner[c][p] + cross[c][:, cols]
                s_lo = jnp.sum(jnp.where(low, o, 0.0), axis=-1, keepdims=True)
                s_hi = jnp.sum(jnp.where(low, 0.0, o), axis=-1, keepdims=True)
                d = o - jnp.where(low, s_lo, s_hi) * (1.0 / RET_DV)
                d2 = d * d
                v_lo = jnp.sum(jnp.where(low, d2, 0.0), axis=-1, keepdims=True)
                v_hi = jnp.sum(jnp.where(low, 0.0, d2), axis=-1, keepdims=True)
                var = jnp.where(low, v_lo, v_hi) * (1.0 / RET_DV)
                y = d * lax.rsqrt(var + NORM_EPS) * gn[:, cols]
                gp = z_ref[rows[c], 2 * LANES + RET_WIDTH + p * LANES:
                           2 * LANES + RET_WIDTH + (p + 1) * LANES]
                o_ref[rows[c], cols] = (gp * jax.nn.sigmoid(gp) * y).astype(BF16)
        return states[group_size]

    lax.fori_loop(0, n_chunks // group_size, group, jnp.zeros((LANES, RET_WIDTH), F32))


def _retention(zr, cos, sin, consts, gn, l):
    B, S, _ = zr.shape
    decay, qd, kd, cd = consts
    full = lambda a: pl.BlockSpec(a.shape, lambda b: (0,) * a.ndim)
    return pl.pallas_call(
        functools.partial(_retention_kernel, n_chunks=S // RET_CHUNK,
                          group_size=RET_GROUP),
        grid=(B,),
        in_specs=[pl.BlockSpec((None, S, Z_RET), lambda b: (b, 0, 0)),
                  pl.BlockSpec((None, S, LANES), lambda b: (b, 0, 0)),
                  pl.BlockSpec((None, S, LANES), lambda b: (b, 0, 0)),
                  full(decay), full(qd), full(kd), full(cd),
                  pl.BlockSpec((None, 1, RET_WIDTH), lambda b: (l, 0, 0))],
        out_specs=pl.BlockSpec((None, S, RET_WIDTH), lambda b: (b, 0, 0)),
        out_shape=jax.ShapeDtypeStruct((B, S, RET_WIDTH), BF16),
        compiler_params=_params("parallel"),
        name="retention",
    )(zr, cos, sin, decay, qd, kd, cd, gn)


def _out_ffn_kernel(x_ref, ya_ref, yb_ref, yc_ref, mod_ref, g_ref, wo_ref, wg_ref, wu_ref,
                    wd_ref, fmod_ref, fg_ref, o_ref, h_s, acc_s, *, final):
    B, tt, D = x_ref.shape
    rows = B * tt
    mod = mod_ref[...]
    a0, b0 = MLA_WIDTH, MLA_WIDTH + RET_WIDTH
    yc = pltpu.einshape("tbd->btd", yc_ref[...]).reshape(rows, LRU_WIDTH).astype(BF16)
    y = (jnp.dot(ya_ref[...].reshape(rows, MLA_WIDTH), wo_ref[0:a0, :],
                 preferred_element_type=F32)
         + jnp.dot(yb_ref[...].reshape(rows, RET_WIDTH), wo_ref[a0:b0, :],
                   preferred_element_type=F32)
         + jnp.dot(yc, wo_ref[b0:, :], preferred_element_type=F32))
    x1 = x_ref[...] + mod[:, 2:3] * y.reshape(B, tt, D)
    o_ref[...] = x1
    h = _rms(x1, g_ref[...]) * (1.0 + mod[:, 4:5]) + mod[:, 3:4]
    h_s[...] = h.reshape(rows, D).astype(BF16)

    def gate_up(j):
        cols = slice(j * FFN_CHUNK, (j + 1) * FFN_CHUNK)
        h = h_s[...]
        return (jnp.dot(h, wg_ref[:, cols], preferred_element_type=F32),
                jnp.dot(h, wu_ref[:, cols], preferred_element_type=F32))

    n_chunks = FFN_HIDDEN // FFN_CHUNK
    nxt = gate_up(0)
    for j in range(n_chunks):
        gate, up = nxt
        if j + 1 < n_chunks:
            nxt = gate_up(j + 1)
        act = (gate * jax.nn.sigmoid(gate) * up).astype(BF16)
        down = jnp.dot(act, wd_ref[j * FFN_CHUNK:(j + 1) * FFN_CHUNK, :],
                       preferred_element_type=F32)
        acc_s[...] = down if j == 0 else acc_s[...] + down
    x2 = o_ref[...] + mod[:, 5:6] * acc_s[...].reshape(B, tt, D)
    if final:
        fmod = fmod_ref[...]
        x2 = _rms(x2, fg_ref[...]) * (1.0 + fmod[:, 1:2]) + fmod[:, 0:1]
    o_ref[...] = x2


def _out_ffn(x, ya, yb, yc, mod, norm, wo, wgu, wd, fmod, fnorm, l, tt, final):
    B, S, D = x.shape
    tok = lambda width: pl.BlockSpec((B, tt, width), lambda i: (0, i, 0))
    return pl.pallas_call(
        functools.partial(_out_ffn_kernel, final=final),
        grid=(S // tt,),
        in_specs=[tok(D), tok(MLA_WIDTH), tok(RET_WIDTH),
                  pl.BlockSpec((tt, B, LRU_WIDTH), lambda i: (i, 0, 0)),
                  pl.BlockSpec((None, B, N_MOD, D), lambda i: (l, 0, 0, 0)),
                  pl.BlockSpec((None, 1, D), lambda i: (l, 0, 0)),
                  _resident((None, D, D), lambda i: (l, 0, 0)),
                  _resident((None, D, FFN_HIDDEN), lambda i: (l, 0, 0)),
                  _resident((None, D, FFN_HIDDEN), lambda i: (l, 0, 1)),
                  _resident((None, FFN_HIDDEN, D), lambda i: (l, 0, 0)),
                  pl.BlockSpec((B, 2, D), lambda i: (0, 0, 0)),
                  pl.BlockSpec((1, D), lambda i: (0, 0))],
        out_specs=tok(D),
        out_shape=jax.ShapeDtypeStruct((B, S, D), F32),
        scratch_shapes=[pltpu.VMEM((B * tt, D), BF16), pltpu.VMEM((B * tt, D), F32)],
        compiler_params=_params("parallel"),
        name="out_ffn",
    )(x, ya, yb, yc, mod, norm, wo, wgu, wgu, wd, fmod, fnorm)


def _split_in_proj(w_in):
    n_mla = MLA_Q_RANK + MLA_KV_RANK + MLA_ROPE
    w_mla = jnp.pad(w_in[..., :n_mla].astype(BF16), ((0, 0), (0, 0), (0, Z_MLA - n_mla)))
    return (w_mla, w_in[..., n_mla:n_mla + Z_RET].astype(BF16),
            w_in[..., n_mla + Z_RET:].astype(BF16))


def _pad_heads(w, width):
    L, R, H, _ = w.shape
    return jnp.pad(w, ((0, 0), (0, 0), (0, 0), (0, HEAD_PAD - width))).reshape(L, R, H * HEAD_PAD)


def _mla_weights(w_uq, w_ukv):
    L = w_uq.shape[0]
    wq = _pad_heads(w_uq.reshape(L, MLA_Q_RANK, MLA_HEADS, MLA_NOPE + MLA_ROPE), MLA_NOPE + MLA_ROPE)
    kv = w_ukv.reshape(L, MLA_KV_RANK, MLA_HEADS, MLA_NOPE + MLA_V)
    wk = _pad_heads(kv[..., :MLA_NOPE], MLA_NOPE)
    v = kv[..., MLA_NOPE:]
    odd = (jnp.arange(MLA_HEADS) % 2 == 1)[None, None, :, None]
    zero = jnp.zeros_like(v)
    wv = jnp.concatenate([jnp.where(odd, zero, v), jnp.where(odd, v, zero)], axis=-1)
    wv = wv.reshape(L, MLA_KV_RANK, MLA_PAD)
    r = jnp.arange(LANES)[:, None]
    c = jnp.arange(MLA_PAD)[None, :]
    wr = ((r < MLA_ROPE) & (c % HEAD_PAD == MLA_NOPE + r)).astype(BF16)
    ones = ((c % HEAD_PAD) == jnp.where((c // HEAD_PAD) % 2 == 0, MLA_V, 0)).astype(F32)
    wqt = jnp.swapaxes(wq, 1, 2).astype(BF16)
    wvt = jnp.swapaxes(wv, 1, 2).astype(BF16)
    return wqt, wk.astype(BF16), wr, wvt, ones.reshape(MLA_PAD, 1)


def _block_diag(w):
    L, G, I, J = w.shape
    eye = jnp.eye(G, dtype=w.dtype)
    return jnp.einsum('lgij,gh->lgihj', w, eye).reshape(L, G * I, G * J)


def kernel(x, c, positions, mod_w, mod_b, norm1, w_in, mla_q_norm, mla_w_uq, mla_kv_norm, mla_w_ukv, ret_gn, lru_conv_w, lru_conv_b, lru_w_a, lru_b_a, lru_w_i, lru_b_i, lru_lambda, w_out, norm2, w_gate_up, w_down, final_norm, final_mod_w, final_mod_b):
    B, S, D = x.shape
    L = mod_w.shape[0]
    tm = min(S, 512)
    tt = tm // B
    tq = min(S, 256)
    row = lambda a: a.reshape(L, 1, a.shape[-1])

    mod = _modulation(c, mod_w, mod_b, 1536).reshape(L, B, N_MOD, D)
    fmod = _modulation(c, final_mod_w[None], final_mod_b[None], 1024).reshape(B, 2, D)
    fnorm = final_norm.reshape(1, D)
    tables = _rope_tables(positions)

    w_in_groups = _split_in_proj(w_in)
    wqt, wk, wr, wvt, ones = _mla_weights(mla_w_uq, mla_w_ukv)
    wa = _block_diag(lru_w_a).astype(BF16)
    wi = _block_diag(lru_w_i).astype(BF16)
    wo = w_out.astype(BF16)
    wgu = w_gate_up.astype(BF16)
    wd = w_down.astype(BF16)
    ret_consts = _retention_consts()

    for l in range(L):
        zm, zr, yc = _in_proj_lru(x, mod, row(norm1), w_in_groups, lru_conv_w, row(lru_conv_b), wa,
                                  row(lru_b_a), wi, row(lru_b_i), row(lru_lambda), l, tt)
        qt, k, vt = _mla_prep(zm, tables, mla_q_norm, mla_kv_norm, wqt, wk, wr, wvt, ones,
                              l, tm, tq)
        ya = _attention(qt, k, vt, tq)
        yb = _retention(zr, tables[0], tables[1], ret_consts, row(ret_gn), l)
        x = _out_ffn(x, ya, yb, yc, mod, row(norm2), wo, wgu, wd, fmod, fnorm, l, tt,
                     final=(l == L - 1))
    return x
```

```python
import functools

import jax
import jax.numpy as jnp
from jax import lax
from jax.experimental import pallas as pl
from jax.experimental.pallas import tpu as pltpu

D_MODEL = 1024
MLA_HEADS = 6
MLA_Q_RANK = 256
MLA_KV_RANK = 128
MLA_NOPE = 64
MLA_ROPE = 32
MLA_V = 64
MLA_WIDTH = MLA_HEADS * MLA_V
RET_HEADS = 4
RET_DK = 32
RET_DV = 64
RET_WIDTH = RET_HEADS * RET_DV
RET_CHUNK = 128
RET_GROUP = 4
LRU_WIDTH = D_MODEL - MLA_WIDTH - RET_WIDTH
LRU_BLOCKS = 6
LRU_BLOCK = LRU_WIDTH // LRU_BLOCKS
CONV_WIDTH = 4
LRU_C = 8.0
FFN_HIDDEN = 2816
ROPE_BASE = 10000.0
NORM_EPS = 1e-6
LOG2_E = 1.4426950408889634
N_MOD = 6

LANES = 128
HEAD_PAD = LANES
MLA_PAD = MLA_HEADS * HEAD_PAD
Z_MLA = 512
Z_RET = 2 * RET_HEADS * RET_DK + 2 * RET_WIDTH
Z_LRU = 2 * LRU_WIDTH
FFN_CHUNK = 256
VMEM_LIMIT = 56 * 1024 * 1024

BF16 = jnp.bfloat16
F32 = jnp.float32


def _params(*sem):
    return pltpu.CompilerParams(dimension_semantics=sem, vmem_limit_bytes=VMEM_LIMIT)


def _resident(shape, index_map):
    return pl.BlockSpec(shape, index_map, pipeline_mode=pl.Buffered(1))


def _rms(x, g):
    return x * lax.rsqrt(jnp.mean(x * x, axis=-1, keepdims=True) + NORM_EPS) * g


def _swap_halves(x, first_half):
    return jnp.where(first_half, pltpu.roll(x, LANES - 16, 1), pltpu.roll(x, 16, 1))


def _mod_kernel(c_ref, w_ref, b_ref, o_ref):
    c = c_ref[...]
    ca = (c * jax.nn.sigmoid(c)).astype(BF16)
    o_ref[...] = jnp.dot(ca, w_ref[...].astype(BF16), preferred_element_type=F32) + b_ref[...]


def _modulation(c, w, b, tn):
    L, D, N = w.shape
    B = c.shape[0]
    return pl.pallas_call(
        _mod_kernel,
        grid=(L, N // tn),
        in_specs=[pl.BlockSpec((B, D), lambda l, j: (0, 0)),
                  pl.BlockSpec((None, D, tn), lambda l, j: (l, 0, j)),
                  pl.BlockSpec((None, 1, tn), lambda l, j: (l, 0, j))],
        out_specs=pl.BlockSpec((None, B, tn), lambda l, j: (l, 0, j)),
        out_shape=jax.ShapeDtypeStruct((L, B, N), F32),
        compiler_params=_params("parallel", "parallel"),
        name="modulation",
    )(c, w, b.reshape(L, 1, N))


def _rope_table_kernel(pos_ref, inv_ref, cos_ref, sin_ref, cost_ref, sint_ref):
    ang = inv_ref[...] * pos_ref[...].astype(F32)
    cos = jnp.cos(ang)
    sin = jnp.sin(ang)
    cost_ref[...] = cos
    sint_ref[...] = sin
    reps = LANES // (2 * cos.shape[0])
    cos_ref[...] = jnp.concatenate([cos] * (2 * reps), axis=0).T
    sin_ref[...] = jnp.concatenate([-sin, sin] * reps, axis=0).T


def _rope_tables(positions):
    B, S = positions.shape
    half = MLA_ROPE // 2
    inv = ROPE_BASE ** (-jnp.arange(half, dtype=F32) / half)
    ts = min(S, 1024)
    spec = pl.BlockSpec((None, ts, LANES), lambda b, i: (b, i, 0))
    spec_t = pl.BlockSpec((None, half, ts), lambda b, i: (b, 0, i))
    return pl.pallas_call(
        _rope_table_kernel,
        grid=(B, S // ts),
        in_specs=[pl.BlockSpec((None, 1, ts), lambda b, i: (b, 0, i)),
                  pl.BlockSpec((half, 1), lambda b, i: (0, 0))],
        out_specs=[spec, spec, spec_t, spec_t],
        out_shape=[jax.ShapeDtypeStruct((B, S, LANES), F32)] * 2
        + [jax.ShapeDtypeStruct((B, half, S), F32)] * 2,
        compiler_params=_params("parallel", "parallel"),
        name="rope_tables",
    )(positions.reshape(B, 1, S), inv.reshape(half, 1))


def _softplus(x):
    return jnp.maximum(x, 0.0) + jnp.log1p(jnp.exp(-jnp.abs(x)))


def _gelu_tanh(x):
    return 0.5 * x * (1.0 + jnp.tanh(0.7978845608028654 * (x + 0.044715 * (x * x * x))))


def _sigmoid(x):
    return 0.5 * jnp.tanh(0.5 * x) + 0.5


def _in_proj_lru_kernel(x_ref, mod_ref, g_ref, wm_ref, wr_ref, wl_ref, cw_ref, cb_ref, wa_ref,
                        ba_ref, wi_ref, bi_ref, lam_ref, zm_ref, zr_ref, yc_ref, xbuf, h_s,
                        *, halves):
    B, tt, D = x_ref.shape
    th = tt // halves
    W = LRU_WIDTH
    PAD = 8
    taps = CONV_WIDTH - 1

    @pl.when(pl.program_id(0) == 0)
    def _():
        xbuf[0:PAD] = jnp.zeros((PAD, B, W), F32)
        h_s[...] = jnp.zeros((B, W), F32)

    mod = mod_ref[...]
    cw = cw_ref[...]
    log_a_rate = -LRU_C * _softplus(-lam_ref[...])

    def half(k):
        tok = slice(k * th, (k + 1) * th)
        h = _rms(x_ref[:, tok, :], g_ref[...]) * (1.0 + mod[:, 1:2]) + mod[:, 0:1]
        h = h.reshape(B * th, D).astype(BF16)

        zl = jnp.dot(h, wl_ref[...], preferred_element_type=F32)
        zm_ref[:, tok, :] = jnp.dot(h, wm_ref[...],
                                    preferred_element_type=F32).reshape(B, th, Z_MLA)
        zl = pltpu.einshape("btd->tbd", zl.reshape(B, th, Z_LRU))
        gate = _gelu_tanh(zl[:, :, W:])

        xbuf[PAD:PAD + th] = zl[:, :, :W]
        xc = cb_ref[...].reshape(1, 1, W) + sum(
            xbuf[PAD - taps + j:PAD - taps + j + th] * cw[j].reshape(1, 1, W)
            for j in range(CONV_WIDTH))
        xbuf[PAD - taps:PAD] = xbuf[PAD + th - taps:PAD + th]

        xc2 = xc.reshape(th * B, W)
        xb = xc2.astype(BF16)
        r = _sigmoid(jnp.dot(xb, wa_ref[...], preferred_element_type=F32) + ba_ref[...])
        i = _sigmoid(jnp.dot(xb, wi_ref[...], preferred_element_type=F32) + bi_ref[...])
        zr_ref[:, tok, :] = jnp.dot(h, wr_ref[...],
                                    preferred_element_type=F32).reshape(B, th, Z_RET)

        log_a = log_a_rate * r
        a = jnp.exp(log_a)
        one_minus_a2 = -jnp.tanh(log_a) * (a * a + 1.0)
        root = jnp.where(one_minus_a2 > 0.0, one_minus_a2 * lax.rsqrt(one_minus_a2), 0.0)
        a = a.reshape(th, B, W)
        b = (root * i * xc2).reshape(th, B, W)

        hid = h_s[...]
        for t in range(th):
            hid = a[t] * hid + b[t]
            yc_ref[k * th + t] = gate[t] * hid
        h_s[...] = hid

    for k in range(halves):
        half(k)


def _in_proj_lru(x, mod, norm, w_groups, cw, cb, wa, ba, wi, bi, lam, l, tt, halves):
    B, S, D = x.shape
    W = LRU_WIDTH
    tok = lambda width: pl.BlockSpec((B, tt, width), lambda i: (0, i, 0))
    vec = pl.BlockSpec((None, 1, W), lambda i: (l, 0, 0))
    mat = _resident((None, W, W), lambda i: (l, 0, 0))
    return pl.pallas_call(
        functools.partial(_in_proj_lru_kernel, halves=halves),
        grid=(S // tt,),
        in_specs=[tok(D),
                  pl.BlockSpec((None, B, N_MOD, D), lambda i: (l, 0, 0, 0)),
                  pl.BlockSpec((None, 1, D), lambda i: (l, 0, 0)),
                  *[_resident((None, D, w.shape[-1]), lambda i: (l, 0, 0)) for w in w_groups],
                  pl.BlockSpec((None, CONV_WIDTH, W), lambda i: (l, 0, 0)),
                  vec, mat, vec, mat, vec, vec],
        out_specs=[tok(Z_MLA), tok(Z_RET),
                   pl.BlockSpec((tt, B, W), lambda i: (i, 0, 0))],
        out_shape=[jax.ShapeDtypeStruct((B, S, Z_MLA), F32),
                   jax.ShapeDtypeStruct((B, S, Z_RET), F32),
                   jax.ShapeDtypeStruct((S, B, W), F32)],
        scratch_shapes=[pltpu.VMEM((tt // halves + 8, B, W), F32), pltpu.VMEM((B, W), F32)],
        compiler_params=_params("arbitrary"),
        name="in_proj_lru",
    )(x, mod, norm, *w_groups, cw, cb, wa, ba, wi, bi, lam)


def _rms_feature_major(x, g):
    return x * lax.rsqrt(jnp.mean(x * x, axis=0, keepdims=True) + NORM_EPS) * g


def _mla_prep_kernel(z_ref, cos_ref, sin_ref, cost_ref, sint_ref, qn_ref, kvn_ref, kvnr_ref,
                     wqt_ref, wk_ref, wr_ref, wvt_ref, one_ref, qt_ref, k_ref, vt_ref, *, tq):
    z = z_ref[...]
    cos = cos_ref[...]
    sin = sin_ref[...]
    lane = lax.broadcasted_iota(jnp.int32, cos.shape, 1)
    first_half = (lane % 32) < 16

    c_kv = _rms(z[:, MLA_Q_RANK:MLA_Q_RANK + MLA_KV_RANK], kvnr_ref[...]).astype(BF16)
    kr = z[:, MLA_Q_RANK + MLA_KV_RANK:]
    kr = jnp.where(lane < MLA_ROPE, kr * cos + _swap_halves(kr, first_half) * sin, 0.0)
    k = (jnp.dot(c_kv, wk_ref[...], preferred_element_type=F32)
         + jnp.dot(kr.astype(BF16), wr_ref[...], preferred_element_type=F32))
    k_ref[...] = k.astype(BF16)

    zt = z[:, :MLA_Q_RANK + MLA_KV_RANK].T
    c_q = _rms_feature_major(zt[:MLA_Q_RANK], qn_ref[...]).astype(BF16)
    q = jnp.dot(wqt_ref[...], c_q, preferred_element_type=F32)
    qk_scale = (MLA_NOPE + MLA_ROPE) ** -0.5 * LOG2_E
    cos_t = cost_ref[...] * qk_scale
    sin_t = sint_ref[...] * qk_scale
    half = MLA_ROPE // 2
    rows = []
    for h in range(MLA_HEADS):
        r0 = h * HEAD_PAD + MLA_NOPE
        x1 = q[r0:r0 + half]
        x2 = q[r0 + half:r0 + MLA_ROPE]
        rows += [q[h * HEAD_PAD:r0] * qk_scale, x1 * cos_t - x2 * sin_t, x2 * cos_t + x1 * sin_t,
                 q[r0 + MLA_ROPE:(h + 1) * HEAD_PAD]]
    q = jnp.concatenate(rows, axis=0).astype(BF16)
    c_kv_t = _rms_feature_major(zt[MLA_Q_RANK:], kvn_ref[...]).astype(BF16)
    v = (jnp.dot(wvt_ref[...], c_kv_t, preferred_element_type=F32) + one_ref[...]).astype(BF16)
    for t in range(z.shape[0] // tq):
        qt_ref[t] = q[:, t * tq:(t + 1) * tq]
        vt_ref[t] = v[:, t * tq:(t + 1) * tq]


def _mla_prep(zm, tables, qn, kvn, wqt, wk, wr, wvt, ones, l, tm, tq):
    B, S, _ = zm.shape
    cos, sin, cos_t, sin_t = tables
    tok = lambda w: pl.BlockSpec((None, tm, w), lambda b, i: (b, i, 0))
    tab_t = pl.BlockSpec((None, MLA_ROPE // 2, tm), lambda b, i: (b, 0, i))
    col = lambda n: pl.BlockSpec((None, n, 1), lambda b, i: (l, 0, 0))
    tr_spec = pl.BlockSpec((None, tm // tq, MLA_PAD, tq), lambda b, i: (b, i, 0, 0))
    tr_shape = jax.ShapeDtypeStruct((B, S // tq, MLA_PAD, tq), BF16)
    L = qn.shape[0]
    return pl.pallas_call(
        functools.partial(_mla_prep_kernel, tq=tq),
        grid=(B, S // tm),
        in_specs=[tok(Z_MLA), tok(LANES), tok(LANES), tab_t, tab_t,
                  col(MLA_Q_RANK), col(MLA_KV_RANK),
                  pl.BlockSpec((None, 1, MLA_KV_RANK), lambda b, i: (l, 0, 0)),
                  _resident((None, MLA_PAD, MLA_Q_RANK), lambda b, i: (l, 0, 0)),
                  _resident((None, MLA_KV_RANK, MLA_PAD), lambda b, i: (l, 0, 0)),
                  _resident((LANES, MLA_PAD), lambda b, i: (0, 0)),
                  _resident((None, MLA_PAD, MLA_KV_RANK), lambda b, i: (l, 0, 0)),
                  pl.BlockSpec((MLA_PAD, 1), lambda b, i: (0, 0))],
        out_specs=[tr_spec, tok(MLA_PAD), tr_spec],
        out_shape=[tr_shape, jax.ShapeDtypeStruct((B, S, MLA_PAD), BF16), tr_shape],
        compiler_params=_params("parallel", "parallel"),
        name="mla_prep",
    )(zm, cos, sin, cos_t, sin_t, qn.reshape(L, -1, 1), kvn.reshape(L, -1, 1),
      kvn.reshape(L, 1, -1), wqt, wk, wr, wvt, ones)


def _attn_kernel(qt_ref, k_ref, vt_ref, o_ref, s_s, m_s, acc_s, *, tq, nq):
    groups = tq // 8
    key = lax.broadcasted_iota(jnp.int32, (tq, tq), 0)
    qry = lax.broadcasted_iota(jnp.int32, (tq, tq), 1)
    causal = key <= qry
    low = lax.broadcasted_iota(jnp.int32, (HEAD_PAD, tq), 0) < MLA_V
    hs = lambda h: slice(h * HEAD_PAD, (h + 1) * HEAD_PAD)

    def query_block(i):
        for j in range(i + 1):
            for h in range(MLA_HEADS):
                s = jnp.dot(k_ref[j * tq:(j + 1) * tq, hs(h)], qt_ref[i, hs(h), :],
                            preferred_element_type=F32)
                if j == i:
                    s = jnp.where(causal, s, -jnp.inf)
                s_s[h, j] = s
                m = jnp.max(s.reshape(groups, 8, tq), axis=0)
                m_s[h] = m if j == 0 else jnp.maximum(m_s[h], m)
        for h in range(MLA_HEADS):
            m_s[h] = jnp.broadcast_to(jnp.max(m_s[h], axis=0, keepdims=True), (8, tq))
        for j in range(i + 1):
            for h in range(MLA_HEADS):
                p = jnp.exp2(s_s[h, j].reshape(groups, 8, tq) - m_s[h][None])
                pv = jnp.dot(vt_ref[j, hs(h), :], p.reshape(tq, tq).astype(BF16),
                             preferred_element_type=F32)
                acc_s[h] = pv if j == 0 else acc_s[h] + pv

        for p in range(MLA_HEADS // 2):
            even = acc_s[2 * p]
            odd = acc_s[2 * p + 1]
            pair = jnp.where(low, even / even[MLA_V:MLA_V + 1], odd / odd[0:1])
            o_ref[i * tq:(i + 1) * tq, p * LANES:(p + 1) * LANES] = pair.T.astype(BF16)

    def body(i, carry):
        lax.switch(i, [functools.partial(query_block, n) for n in range(nq)])
        return carry

    lax.fori_loop(0, nq, body, 0)


def _attention(qt, k, vt, tq):
    B, S, _ = k.shape
    nq = S // tq
    tr_spec = pl.BlockSpec((None, nq, MLA_PAD, tq), lambda b: (b, 0, 0, 0))
    return pl.pallas_call(
        functools.partial(_attn_kernel, tq=tq, nq=nq),
        grid=(B,),
        in_specs=[tr_spec, pl.BlockSpec((None, S, MLA_PAD), lambda b: (b, 0, 0)), tr_spec],
        out_specs=pl.BlockSpec((None, S, MLA_WIDTH), lambda b: (b, 0, 0)),
        out_shape=jax.ShapeDtypeStruct((B, S, MLA_WIDTH), BF16),
        scratch_shapes=[pltpu.VMEM((MLA_HEADS, nq, tq, tq), F32),
                        pltpu.VMEM((MLA_HEADS, 8, tq), F32),
                        pltpu.VMEM((MLA_HEADS, HEAD_PAD, tq), F32)],
        compiler_params=_params("parallel"),
        name="mla_attention",
    )(qt, k, vt)


def _retention_consts():
    C, H = RET_CHUNK, RET_HEADS
    log_g = jnp.log(1.0 - jnp.exp2(-5.0 - jnp.arange(H, dtype=F32)))
    idx = jnp.arange(C, dtype=F32)
    diff = idx[:, None] - idx[None, :]
    decay = jnp.where(diff >= 0, jnp.exp(log_g[:, None, None] * jnp.maximum(diff, 0.0)), 0.0)
    q_decay = jnp.exp(log_g[:, None] * (idx + 1.0))
    k_decay = jnp.exp(log_g[:, None] * (C - 1.0 - idx))
    chunk_decay = jnp.exp(log_g * C)
    qd = jnp.repeat(q_decay.T, RET_DK, axis=1)
    kd = jnp.repeat(k_decay.T, RET_DK, axis=1)
    cd = jnp.broadcast_to(jnp.repeat(chunk_decay, RET_DK)[:, None], (H * RET_DK, RET_WIDTH))
    return decay, qd, kd, cd


def _retention_kernel(z_ref, cos_ref, sin_ref, dec_ref, qd_ref, kd_ref, cd_ref, gn_ref, o_ref,
                      *, n_chunks, group_size):
    C = RET_CHUNK
    lane = lax.broadcasted_iota(jnp.int32, (C, LANES), 1)
    first_half = (lane % 32) < 16
    low = lane < RET_DV
    head_of_lane = lane // RET_DK
    srow = lax.broadcasted_iota(jnp.int32, (LANES, RET_WIDTH), 0) // RET_DK
    scol = lax.broadcasted_iota(jnp.int32, (LANES, RET_WIDTH), 1) // RET_DV
    same_head = srow == scol
    gn = gn_ref[...]
    cd = cd_ref[...]

    def rotate(x, rows):
        return x * cos_ref[rows, :] + _swap_halves(x, first_half) * sin_ref[rows, :]

    def group(n, state):
        rows = [pl.ds(pl.multiple_of((n * group_size + c) * C, C), C) for c in range(group_size)]
        q = [rotate(z_ref[r, 0:LANES], r) for r in rows]
        k = [rotate(z_ref[r, LANES:2 * LANES], r) * (RET_DK ** -0.5) for r in rows]
        kb = [x.astype(BF16) for x in k]
        v = [z_ref[r, 2 * LANES:2 * LANES + RET_WIDTH].astype(BF16) for r in rows]

        s = [[(lax.dot_general(jnp.where(head_of_lane == h, q[c], 0.0).astype(BF16), kb[c],
                               (((1,), (1,)), ((), ())), preferred_element_type=F32)
               * dec_ref[h]).astype(BF16) for h in range(RET_HEADS)] for c in range(group_size)]
        inner = [[jnp.where(low,
                            jnp.dot(s[c][2 * p], v[c][:, p * LANES:(p + 1) * LANES],
                                    preferred_element_type=F32),
                            jnp.dot(s[c][2 * p + 1], v[c][:, p * LANES:(p + 1) * LANES],
                                    preferred_element_type=F32))
                  for p in range(RET_HEADS // 2)] for c in range(group_size)]
        kv = [lax.dot_general((k[c] * kd_ref[...]).astype(BF16), v[c], (((0,), (0,)), ((), ())),
                              preferred_element_type=F32) for c in range(group_size)]
        states = [state]
        for c in range(group_size):
            states.append(states[c] * cd + jnp.where(same_head, kv[c], 0.0))
        cross = [jnp.dot((q[c] * qd_ref[...]).astype(BF16), states[c].astype(BF16),
                         preferred_element_type=F32) for c in range(group_size)]

        for c in range(group_size):
            for p in range(RET_HEADS // 2):
                cols = slice(p * LANES, (p + 1) * LANES)
                o = inner[c][p] + cross[c][:, cols]
                s_lo = jnp.sum(jnp.where(low, o, 0.0), axis=-1, keepdims=True)
                s_hi = jnp.sum(jnp.where(low, 0.0, o), axis=-1, keepdims=True)
                d = o - jnp.where(low, s_lo, s_hi) * (1.0 / RET_DV)
                d2 = d * d
                v_lo = jnp.sum(jnp.where(low, d2, 0.0), axis=-1, keepdims=True)
                v_hi = jnp.sum(jnp.where(low, 0.0, d2), axis=-1, keepdims=True)
                var = jnp.where(low, v_lo, v_hi) * (1.0 / RET_DV)
                y = d * lax.rsqrt(var + NORM_EPS) * gn[:, cols]
                gp = z_ref[rows[c], 2 * LANES + RET_WIDTH + p * LANES:
                           2 * LANES + RET_WIDTH + (p + 1) * LANES]
                o_ref[rows[c], cols] = (gp * jax.nn.sigmoid(gp) * y).astype(BF16)
        return states[group_size]

    lax.fori_loop(0, n_chunks // group_size, group, jnp.zeros((LANES, RET_WIDTH), F32))


def _retention(zr, cos, sin, consts, gn, l):
    B, S, _ = zr.shape
    decay, qd, kd, cd = consts
    full = lambda a: pl.BlockSpec(a.shape, lambda b: (0,) * a.ndim)
    return pl.pallas_call(
        functools.partial(_retention_kernel, n_chunks=S // RET_CHUNK,
                          group_size=RET_GROUP),
        grid=(B,),
        in_specs=[pl.BlockSpec((None, S, Z_RET), lambda b: (b, 0, 0)),
                  pl.BlockSpec((None, S, LANES), lambda b: (b, 0, 0)),
                  pl.BlockSpec((None, S, LANES), lambda b: (b, 0, 0)),
                  full(decay), full(qd), full(kd), full(cd),
                  pl.BlockSpec((None, 1, RET_WIDTH), lambda b: (l, 0, 0))],
        out_specs=pl.BlockSpec((None, S, RET_WIDTH), lambda b: (b, 0, 0)),
        out_shape=jax.ShapeDtypeStruct((B, S, RET_WIDTH), BF16),
        compiler_params=_params("parallel"),
        name="retention",
    )(zr, cos, sin, decay, qd, kd, cd, gn)


def _out_ffn_kernel(x_ref, ya_ref, yb_ref, yc_ref, mod_ref, g_ref, wo_ref, wg_ref, wu_ref,
                    wd_ref, fmod_ref, fg_ref, o_ref, h_s, acc_s, *, final, halves):
    B, tt, D = x_ref.shape
    th = tt // halves
    rows = B * th
    mod = mod_ref[...]
    a0, b0 = MLA_WIDTH, MLA_WIDTH + RET_WIDTH
    n_chunks = FFN_HIDDEN // FFN_CHUNK

    def out_proj(k):
        tok = slice(k * th, (k + 1) * th)
        r = slice(k * rows, (k + 1) * rows)
        yc = pltpu.einshape("tbd->btd", yc_ref[tok]).reshape(rows, LRU_WIDTH).astype(BF16)
        y = (jnp.dot(ya_ref[:, tok, :].reshape(rows, MLA_WIDTH), wo_ref[0:a0, :],
                     preferred_element_type=F32)
             + jnp.dot(yb_ref[:, tok, :].reshape(rows, RET_WIDTH), wo_ref[a0:b0, :],
                       preferred_element_type=F32)
             + jnp.dot(yc, wo_ref[b0:, :], preferred_element_type=F32))
        x1 = x_ref[:, tok, :] + mod[:, 2:3] * y.reshape(B, th, D)
        o_ref[:, tok, :] = x1
        h = _rms(x1, g_ref[...]) * (1.0 + mod[:, 4:5]) + mod[:, 3:4]
        h_s[r, :] = h.reshape(rows, D).astype(BF16)

    def ffn(k):
        tok = slice(k * th, (k + 1) * th)
        r = slice(k * rows, (k + 1) * rows)

        def gate_up(j):
            cols = slice(j * FFN_CHUNK, (j + 1) * FFN_CHUNK)
            h = h_s[r, :]
            return (jnp.dot(h, wg_ref[:, cols], preferred_element_type=F32),
                    jnp.dot(h, wu_ref[:, cols], preferred_element_type=F32))

        nxt = gate_up(0)
        for j in range(n_chunks):
            gate, up = nxt
            if j + 1 < n_chunks:
                nxt = gate_up(j + 1)
            act = (gate * jax.nn.sigmoid(gate) * up).astype(BF16)
            down = jnp.dot(act, wd_ref[j * FFN_CHUNK:(j + 1) * FFN_CHUNK, :],
                           preferred_element_type=F32)
            acc_s[r, :] = down if j == 0 else acc_s[r, :] + down
        x2 = o_ref[:, tok, :] + mod[:, 5:6] * acc_s[r, :].reshape(B, th, D)
        if final:
            fmod = fmod_ref[...]
            x2 = _rms(x2, fg_ref[...]) * (1.0 + fmod[:, 1:2]) + fmod[:, 0:1]
        o_ref[:, tok, :] = x2

    for k in range(halves):
        out_proj(k)
    for k in range(halves):
        ffn(k)


def _out_ffn(x, ya, yb, yc, mod, norm, wo, wgu, wd, fmod, fnorm, l, tt, halves, final):
    B, S, D = x.shape
    tok = lambda width: pl.BlockSpec((B, tt, width), lambda i: (0, i, 0))
    return pl.pallas_call(
        functools.partial(_out_ffn_kernel, final=final, halves=halves),
        grid=(S // tt,),
        in_specs=[tok(D), tok(MLA_WIDTH), tok(RET_WIDTH),
                  pl.BlockSpec((tt, B, LRU_WIDTH), lambda i: (i, 0, 0)),
                  pl.BlockSpec((None, B, N_MOD, D), lambda i: (l, 0, 0, 0)),
                  pl.BlockSpec((None, 1, D), lambda i: (l, 0, 0)),
                  _resident((None, D, D), lambda i: (l, 0, 0)),
                  _resident((None, D, FFN_HIDDEN), lambda i: (l, 0, 0)),
                  _resident((None, D, FFN_HIDDEN), lambda i: (l, 0, 1)),
                  _resident((None, FFN_HIDDEN, D), lambda i: (l, 0, 0)),
                  pl.BlockSpec((B, 2, D), lambda i: (0, 0, 0)),
                  pl.BlockSpec((1, D), lambda i: (0, 0))],
        out_specs=tok(D),
        out_shape=jax.ShapeDtypeStruct((B, S, D), F32),
        scratch_shapes=[pltpu.VMEM((B * tt, D), BF16), pltpu.VMEM((B * tt, D), F32)],
        compiler_params=_params("parallel"),
        name="out_ffn",
    )(x, ya, yb, yc, mod, norm, wo, wgu, wgu, wd, fmod, fnorm)


def _split_in_proj(w_in):
    n_mla = MLA_Q_RANK + MLA_KV_RANK + MLA_ROPE
    w_mla = jnp.pad(w_in[..., :n_mla].astype(BF16), ((0, 0), (0, 0), (0, Z_MLA - n_mla)))
    return (w_mla, w_in[..., n_mla:n_mla + Z_RET].astype(BF16),
            w_in[..., n_mla + Z_RET:].astype(BF16))


def _pad_heads(w, width):
    L, R, H, _ = w.shape
    return jnp.pad(w, ((0, 0), (0, 0), (0, 0), (0, HEAD_PAD - width))).reshape(L, R, H * HEAD_PAD)


def _mla_weights(w_uq, w_ukv):
    L = w_uq.shape[0]
    wq = _pad_heads(w_uq.reshape(L, MLA_Q_RANK, MLA_HEADS, MLA_NOPE + MLA_ROPE), MLA_NOPE + MLA_ROPE)
    kv = w_ukv.reshape(L, MLA_KV_RANK, MLA_HEADS, MLA_NOPE + MLA_V)
    wk = _pad_heads(kv[..., :MLA_NOPE], MLA_NOPE)
    v = kv[..., MLA_NOPE:]
    odd = (jnp.arange(MLA_HEADS) % 2 == 1)[None, None, :, None]
    zero = jnp.zeros_like(v)
    wv = jnp.concatenate([jnp.where(odd, zero, v), jnp.where(odd, v, zero)], axis=-1)
    wv = wv.reshape(L, MLA_KV_RANK, MLA_PAD)
    r = jnp.arange(LANES)[:, None]
    c = jnp.arange(MLA_PAD)[None, :]
    wr = ((r < MLA_ROPE) & (c % HEAD_PAD == MLA_NOPE + r)).astype(BF16)
    ones = ((c % HEAD_PAD) == jnp.where((c // HEAD_PAD) % 2 == 0, MLA_V, 0)).astype(F32)
    wqt = jnp.swapaxes(wq, 1, 2).astype(BF16)
    wvt = jnp.swapaxes(wv, 1, 2).astype(BF16)
    return wqt, wk.astype(BF16), wr, wvt, ones.reshape(MLA_PAD, 1)


def _block_diag(w):
    L, G, I, J = w.shape
    eye = jnp.eye(G, dtype=w.dtype)
    return jnp.einsum('lgij,gh->lgihj', w, eye).reshape(L, G * I, G * J)


def kernel(x, c, positions, mod_w, mod_b, norm1, w_in, mla_q_norm, mla_w_uq, mla_kv_norm, mla_w_ukv, ret_gn, lru_conv_w, lru_conv_b, lru_w_a, lru_b_a, lru_w_i, lru_b_i, lru_lambda, w_out, norm2, w_gate_up, w_down, final_norm, final_mod_w, final_mod_b):
    B, S, D = x.shape
    L = mod_w.shape[0]
    tm = min(S, 512)
    halves = 2
    tt = halves * tm // B
    tq = min(S, 256)
    row = lambda a: a.reshape(L, 1, a.shape[-1])

    mod = _modulation(c, mod_w, mod_b, 1536).reshape(L, B, N_MOD, D)
    fmod = _modulation(c, final_mod_w[None], final_mod_b[None], 1024).reshape(B, 2, D)
    fnorm = final_norm.reshape(1, D)
    tables = _rope_tables(positions)

    w_in_groups = _split_in_proj(w_in)
    wqt, wk, wr, wvt, ones = _mla_weights(mla_w_uq, mla_w_ukv)
    wa = _block_diag(lru_w_a).astype(BF16)
    wi = _block_diag(lru_w_i).astype(BF16)
    wo = w_out.astype(BF16)
    wgu = w_gate_up.astype(BF16)
    wd = w_down.astype(BF16)
    ret_consts = _retention_consts()

    for l in range(L):
        zm, zr, yc = _in_proj_lru(x, mod, row(norm1), w_in_groups, lru_conv_w, row(lru_conv_b), wa,
                                  row(lru_b_a), wi, row(lru_b_i), row(lru_lambda), l, tt, halves)
        qt, k, vt = _mla_prep(zm, tables, mla_q_norm, mla_kv_norm, wqt, wk, wr, wvt, ones,
                              l, min(S, 2 * tm), tq)
        ya = _attention(qt, k, vt, tq)
        yb = _retention(zr, tables[0], tables[1], ret_consts, row(ret_gn), l)
        x = _out_ffn(x, ya, yb, yc, mod, row(norm2), wo, wgu, wd, fmod, fnorm, l, tt, halves,
                     final=(l == L - 1))
    return x
```

```python
import functools

import jax
import jax.numpy as jnp
from jax import lax
from jax.experimental import pallas as pl
from jax.experimental.pallas import tpu as pltpu

D_MODEL = 1024
MLA_HEADS = 6
MLA_Q_RANK = 256
MLA_KV_RANK = 128
MLA_NOPE = 64
MLA_ROPE = 32
MLA_V = 64
MLA_WIDTH = MLA_HEADS * MLA_V
RET_HEADS = 4
RET_DK = 32
RET_DV = 64
RET_WIDTH = RET_HEADS * RET_DV
RET_CHUNK = 128
LRU_WIDTH = D_MODEL - MLA_WIDTH - RET_WIDTH
LRU_BLOCKS = 6
LRU_BLOCK = LRU_WIDTH // LRU_BLOCKS
CONV_WIDTH = 4
LRU_C = 8.0
FFN_HIDDEN = 2816
ROPE_BASE = 10000.0
NORM_EPS = 1e-6
LOG2_E = 1.4426950408889634
N_MOD = 6

LANES = 128
HEAD_PAD = LANES
MLA_PAD = MLA_HEADS * HEAD_PAD
Z_MLA = 512
Z_RET = 2 * RET_HEADS * RET_DK + 2 * RET_WIDTH
Z_LRU = 2 * LRU_WIDTH
FFN_CHUNK = 256
VMEM_LIMIT = 56 * 1024 * 1024

BF16 = jnp.bfloat16
F32 = jnp.float32


def _params(*sem):
    return pltpu.CompilerParams(dimension_semantics=sem, vmem_limit_bytes=VMEM_LIMIT)


def _resident(shape, index_map):
    return pl.BlockSpec(shape, index_map, pipeline_mode=pl.Buffered(1))


def _rms(x, g):
    return x * lax.rsqrt(jnp.mean(x * x, axis=-1, keepdims=True) + NORM_EPS) * g


def _swap_halves(x, first_half):
    return jnp.where(first_half, pltpu.roll(x, LANES - 16, 1), pltpu.roll(x, 16, 1))


def _mod_kernel(c_ref, w_ref, b_ref, o_ref):
    c = c_ref[...]
    ca = (c * jax.nn.sigmoid(c)).astype(BF16)
    o_ref[...] = jnp.dot(ca, w_ref[...].astype(BF16), preferred_element_type=F32) + b_ref[...]


def _modulation(c, w, b, tn):
    L, D, N = w.shape
    B = c.shape[0]
    return pl.pallas_call(
        _mod_kernel,
        grid=(L, N // tn),
        in_specs=[pl.BlockSpec((B, D), lambda l, j: (0, 0)),
                  pl.BlockSpec((None, D, tn), lambda l, j: (l, 0, j)),
                  pl.BlockSpec((None, 1, tn), lambda l, j: (l, 0, j))],
        out_specs=pl.BlockSpec((None, B, tn), lambda l, j: (l, 0, j)),
        out_shape=jax.ShapeDtypeStruct((L, B, N), F32),
        compiler_params=_params("parallel", "parallel"),
        name="modulation",
    )(c, w, b.reshape(L, 1, N))


def _rope_table_kernel(pos_ref, inv_ref, cos_ref, sin_ref, cost_ref, sint_ref):
    ang = inv_ref[...] * pos_ref[...].astype(F32)
    cos = jnp.cos(ang)
    sin = jnp.sin(ang)
    cost_ref[...] = cos
    sint_ref[...] = sin
    reps = LANES // (2 * cos.shape[0])
    cos_ref[...] = jnp.concatenate([cos] * (2 * reps), axis=0).T
    sin_ref[...] = jnp.concatenate([-sin, sin] * reps, axis=0).T


def _rope_tables(positions):
    B, S = positions.shape
    half = MLA_ROPE // 2
    inv = ROPE_BASE ** (-jnp.arange(half, dtype=F32) / half)
    ts = min(S, 1024)
    spec = pl.BlockSpec((None, ts, LANES), lambda b, i: (b, i, 0))
    spec_t = pl.BlockSpec((None, half, ts), lambda b, i: (b, 0, i))
    return pl.pallas_call(
        _rope_table_kernel,
        grid=(B, S // ts),
        in_specs=[pl.BlockSpec((None, 1, ts), lambda b, i: (b, 0, i)),
                  pl.BlockSpec((half, 1), lambda b, i: (0, 0))],
        out_specs=[spec, spec, spec_t, spec_t],
        out_shape=[jax.ShapeDtypeStruct((B, S, LANES), F32)] * 2
        + [jax.ShapeDtypeStruct((B, half, S), F32)] * 2,
        compiler_params=_params("parallel", "parallel"),
        name="rope_tables",
    )(positions.reshape(B, 1, S), inv.reshape(half, 1))


def _softplus(x):
    return jnp.maximum(x, 0.0) + jnp.log1p(jnp.exp(-jnp.abs(x)))


def _gelu_tanh(x):
    return 0.5 * x * (1.0 + jnp.tanh(0.7978845608028654 * (x + 0.044715 * (x * x * x))))


def _sigmoid(x):
    return 0.5 * jnp.tanh(0.5 * x) + 0.5


def _in_proj_lru_kernel(x_ref, mod_ref, g_ref, wm_ref, wr_ref, wl_ref, cw_ref, cb_ref, wa_ref,
                        ba_ref, wi_ref, bi_ref, lam_ref, zm_ref, zr_ref, yc_ref, xbuf, h_s,
                        *, halves):
    B, tt, D = x_ref.shape
    th = tt // halves
    W = LRU_WIDTH
    PAD = 8
    taps = CONV_WIDTH - 1

    @pl.when(pl.program_id(0) == 0)
    def _():
        xbuf[0:PAD] = jnp.zeros((PAD, B, W), F32)
        h_s[...] = jnp.zeros((B, W), F32)

    mod = mod_ref[...]
    cw = cw_ref[...]
    log_a_rate = -LRU_C * _softplus(-lam_ref[...])

    def half(k):
        tok = slice(k * th, (k + 1) * th)
        h = _rms(x_ref[:, tok, :], g_ref[...]) * (1.0 + mod[:, 1:2]) + mod[:, 0:1]
        h = h.reshape(B * th, D).astype(BF16)

        zl = jnp.dot(h, wl_ref[...], preferred_element_type=F32)
        zm_ref[:, tok, :] = jnp.dot(h, wm_ref[...],
                                    preferred_element_type=F32).reshape(B, th, Z_MLA)
        zl = pltpu.einshape("btd->tbd", zl.reshape(B, th, Z_LRU))
        gate = _gelu_tanh(zl[:, :, W:])

        xbuf[PAD:PAD + th] = zl[:, :, :W]
        xc = cb_ref[...].reshape(1, 1, W) + sum(
            xbuf[PAD - taps + j:PAD - taps + j + th] * cw[j].reshape(1, 1, W)
            for j in range(CONV_WIDTH))
        xbuf[PAD - taps:PAD] = xbuf[PAD + th - taps:PAD + th]

        xc2 = xc.reshape(th * B, W)
        xb = xc2.astype(BF16)
        r = _sigmoid(jnp.dot(xb, wa_ref[...], preferred_element_type=F32) + ba_ref[...])
        i = _sigmoid(jnp.dot(xb, wi_ref[...], preferred_element_type=F32) + bi_ref[...])
        zr_ref[:, tok, :] = jnp.dot(h, wr_ref[...],
                                    preferred_element_type=F32).reshape(B, th, Z_RET)

        log_a = log_a_rate * r
        a = jnp.exp(log_a)
        one_minus_a2 = -jnp.tanh(log_a) * (a * a + 1.0)
        root = jnp.where(one_minus_a2 > 0.0, one_minus_a2 * lax.rsqrt(one_minus_a2), 0.0)
        a = a.reshape(th, B, W)
        b = (root * i * xc2).reshape(th, B, W)

        hid = h_s[...]
        for t in range(th):
            hid = a[t] * hid + b[t]
            yc_ref[k * th + t] = gate[t] * hid
        h_s[...] = hid

    for k in range(halves):
        half(k)


def _in_proj_lru(x, mod, norm, w_groups, cw, cb, wa, ba, wi, bi, lam, l, tt, halves):
    B, S, D = x.shape
    W = LRU_WIDTH
    tok = lambda width: pl.BlockSpec((B, tt, width), lambda i: (0, i, 0))
    vec = pl.BlockSpec((None, 1, W), lambda i: (l, 0, 0))
    mat = _resident((None, W, W), lambda i: (l, 0, 0))
    return pl.pallas_call(
        functools.partial(_in_proj_lru_kernel, halves=halves),
        grid=(S // tt,),
        in_specs=[tok(D),
                  pl.BlockSpec((None, B, N_MOD, D), lambda i: (l, 0, 0, 0)),
                  pl.BlockSpec((None, 1, D), lambda i: (l, 0, 0)),
                  *[_resident((None, D, w.shape[-1]), lambda i: (l, 0, 0)) for w in w_groups],
                  pl.BlockSpec((None, CONV_WIDTH, W), lambda i: (l, 0, 0)),
                  vec, mat, vec, mat, vec, vec],
        out_specs=[tok(Z_MLA), tok(Z_RET),
                   pl.BlockSpec((tt, B, W), lambda i: (i, 0, 0))],
        out_shape=[jax.ShapeDtypeStruct((B, S, Z_MLA), F32),
                   jax.ShapeDtypeStruct((B, S, Z_RET), F32),
                   jax.ShapeDtypeStruct((S, B, W), F32)],
        scratch_shapes=[pltpu.VMEM((tt // halves + 8, B, W), F32), pltpu.VMEM((B, W), F32)],
        compiler_params=_params("arbitrary"),
        name="in_proj_lru",
    )(x, mod, norm, *w_groups, cw, cb, wa, ba, wi, bi, lam)


def _rms_feature_major(x, g):
    return x * lax.rsqrt(jnp.mean(x * x, axis=0, keepdims=True) + NORM_EPS) * g


def _mla_prep_kernel(z_ref, cos_ref, sin_ref, cost_ref, sint_ref, qn_ref, kvn_ref, kvnr_ref,
                     wqt_ref, wk_ref, wr_ref, wvt_ref, one_ref, qt_ref, k_ref, vt_ref, *, tq):
    z = z_ref[...]
    cos = cos_ref[...]
    sin = sin_ref[...]
    lane = lax.broadcasted_iota(jnp.int32, cos.shape, 1)
    first_half = (lane % 32) < 16

    c_kv = _rms(z[:, MLA_Q_RANK:MLA_Q_RANK + MLA_KV_RANK], kvnr_ref[...]).astype(BF16)
    kr = z[:, MLA_Q_RANK + MLA_KV_RANK:]
    kr = jnp.where(lane < MLA_ROPE, kr * cos + _swap_halves(kr, first_half) * sin, 0.0)
    k = (jnp.dot(c_kv, wk_ref[...], preferred_element_type=F32)
         + jnp.dot(kr.astype(BF16), wr_ref[...], preferred_element_type=F32))
    k_ref[...] = k.astype(BF16)

    zt = z[:, :MLA_Q_RANK + MLA_KV_RANK].T
    c_q = _rms_feature_major(zt[:MLA_Q_RANK], qn_ref[...]).astype(BF16)
    q = jnp.dot(wqt_ref[...], c_q, preferred_element_type=F32)
    qk_scale = (MLA_NOPE + MLA_ROPE) ** -0.5 * LOG2_E
    cos_t = cost_ref[...] * qk_scale
    sin_t = sint_ref[...] * qk_scale
    half = MLA_ROPE // 2
    rows = []
    for h in range(MLA_HEADS):
        r0 = h * HEAD_PAD + MLA_NOPE
        x1 = q[r0:r0 + half]
        x2 = q[r0 + half:r0 + MLA_ROPE]
        rows += [q[h * HEAD_PAD:r0] * qk_scale, x1 * cos_t - x2 * sin_t, x2 * cos_t + x1 * sin_t,
                 q[r0 + MLA_ROPE:(h + 1) * HEAD_PAD]]
    q = jnp.concatenate(rows, axis=0).astype(BF16)
    c_kv_t = _rms_feature_major(zt[MLA_Q_RANK:], kvn_ref[...]).astype(BF16)
    v = (jnp.dot(wvt_ref[...], c_kv_t, preferred_element_type=F32) + one_ref[...]).astype(BF16)
    for t in range(z.shape[0] // tq):
        qt_ref[t] = q[:, t * tq:(t + 1) * tq]
        vt_ref[t] = v[:, t * tq:(t + 1) * tq]


def _mla_prep(zm, tables, qn, kvn, wqt, wk, wr, wvt, ones, l, tm, tq):
    B, S, _ = zm.shape
    cos, sin, cos_t, sin_t = tables
    tok = lambda w: pl.BlockSpec((None, tm, w), lambda b, i: (b, i, 0))
    tab_t = pl.BlockSpec((None, MLA_ROPE // 2, tm), lambda b, i: (b, 0, i))
    col = lambda n: pl.BlockSpec((None, n, 1), lambda b, i: (l, 0, 0))
    tr_spec = pl.BlockSpec((None, tm // tq, MLA_PAD, tq), lambda b, i: (b, i, 0, 0))
    tr_shape = jax.ShapeDtypeStruct((B, S // tq, MLA_PAD, tq), BF16)
    L = qn.shape[0]
    return pl.pallas_call(
        functools.partial(_mla_prep_kernel, tq=tq),
        grid=(B, S // tm),
        in_specs=[tok(Z_MLA), tok(LANES), tok(LANES), tab_t, tab_t,
                  col(MLA_Q_RANK), col(MLA_KV_RANK),
                  pl.BlockSpec((None, 1, MLA_KV_RANK), lambda b, i: (l, 0, 0)),
                  _resident((None, MLA_PAD, MLA_Q_RANK), lambda b, i: (l, 0, 0)),
                  _resident((None, MLA_KV_RANK, MLA_PAD), lambda b, i: (l, 0, 0)),
                  _resident((LANES, MLA_PAD), lambda b, i: (0, 0)),
                  _resident((None, MLA_PAD, MLA_KV_RANK), lambda b, i: (l, 0, 0)),
                  pl.BlockSpec((MLA_PAD, 1), lambda b, i: (0, 0))],
        out_specs=[tr_spec, tok(MLA_PAD), tr_spec],
        out_shape=[tr_shape, jax.ShapeDtypeStruct((B, S, MLA_PAD), BF16), tr_shape],
        compiler_params=_params("parallel", "parallel"),
        name="mla_prep",
    )(zm, cos, sin, cos_t, sin_t, qn.reshape(L, -1, 1), kvn.reshape(L, -1, 1),
      kvn.reshape(L, 1, -1), wqt, wk, wr, wvt, ones)


def _retention_consts():
    C, H = RET_CHUNK, RET_HEADS
    log_g = jnp.log(1.0 - jnp.exp2(-5.0 - jnp.arange(H, dtype=F32)))
    idx = jnp.arange(C, dtype=F32)
    diff = idx[:, None] - idx[None, :]
    decay = jnp.where(diff >= 0, jnp.exp(log_g[:, None, None] * jnp.maximum(diff, 0.0)), 0.0)
    q_decay = jnp.exp(log_g[:, None] * (idx + 1.0))
    k_decay = jnp.exp(log_g[:, None] * (C - 1.0 - idx))
    chunk_decay = jnp.exp(log_g * C)
    qd = jnp.repeat(q_decay.T, RET_DK, axis=1)
    kd = jnp.repeat(k_decay.T, RET_DK, axis=1)
    cd = jnp.broadcast_to(jnp.repeat(chunk_decay, RET_DK)[:, None], (H * RET_DK, RET_WIDTH))
    return decay, qd, kd, cd


def _mixers_kernel(qt_ref, k_ref, vt_ref, z_ref, cos_ref, sin_ref, dec_ref, qd_ref, kd_ref,
                   cd_ref, gn_ref, ya_ref, yb_ref, s_s, m_s, acc_s, state_s, *, tq, nq):
    groups = tq // 8
    key = lax.broadcasted_iota(jnp.int32, (tq, tq), 0)
    qry = lax.broadcasted_iota(jnp.int32, (tq, tq), 1)
    causal = key <= qry
    value_rows = lax.broadcasted_iota(jnp.int32, (HEAD_PAD, tq), 0) < MLA_V
    hs = lambda h: slice(h * HEAD_PAD, (h + 1) * HEAD_PAD)

    C = RET_CHUNK
    per_block = tq // C
    lane = lax.broadcasted_iota(jnp.int32, (C, LANES), 1)
    first_half = (lane % 32) < 16
    low = lane < RET_DV
    head_of_lane = lane // RET_DK
    srow = lax.broadcasted_iota(jnp.int32, (LANES, RET_WIDTH), 0) // RET_DK
    scol = lax.broadcasted_iota(jnp.int32, (LANES, RET_WIDTH), 1) // RET_DV
    same_head = srow == scol
    gn = gn_ref[...]
    cd = cd_ref[...]

    def rotate(x, rows):
        return x * cos_ref[rows, :] + _swap_halves(x, first_half) * sin_ref[rows, :]

    def block(i):
        rows = [slice((i * per_block + c) * C, (i * per_block + c + 1) * C)
                for c in range(per_block)]
        q = [rotate(z_ref[r, 0:LANES], r) for r in rows]
        k = [rotate(z_ref[r, LANES:2 * LANES], r) * (RET_DK ** -0.5) for r in rows]
        kb = [x.astype(BF16) for x in k]
        v = [z_ref[r, 2 * LANES:2 * LANES + RET_WIDTH].astype(BF16) for r in rows]

        for j in range(i + 1):
            for h in range(MLA_HEADS):
                s = jnp.dot(k_ref[j * tq:(j + 1) * tq, hs(h)], qt_ref[i, hs(h), :],
                            preferred_element_type=F32)
                if j == i:
                    s = jnp.where(causal, s, -jnp.inf)
                s_s[h, j] = s
                m = jnp.max(s.reshape(groups, 8, tq), axis=0)
                m_s[h] = m if j == 0 else jnp.maximum(m_s[h], m)

        sr = [[(lax.dot_general(jnp.where(head_of_lane == h, q[c], 0.0).astype(BF16), kb[c],
                                (((1,), (1,)), ((), ())), preferred_element_type=F32)
                * dec_ref[h]).astype(BF16) for h in range(RET_HEADS)] for c in range(per_block)]
        kv = [lax.dot_general((k[c] * kd_ref[...]).astype(BF16), v[c], (((0,), (0,)), ((), ())),
                              preferred_element_type=F32) for c in range(per_block)]

        for h in range(MLA_HEADS):
            m_s[h] = jnp.broadcast_to(jnp.max(m_s[h], axis=0, keepdims=True), (8, tq))

        def values(j):
            for h in range(MLA_HEADS):
                p = jnp.exp2(s_s[h, j].reshape(groups, 8, tq) - m_s[h][None])
                pv = jnp.dot(vt_ref[j, hs(h), :], p.reshape(tq, tq).astype(BF16),
                             preferred_element_type=F32)
                acc_s[h] = pv if j == 0 else acc_s[h] + pv

        split = (i + 2) // 2
        for j in range(split):
            values(j)
        inner = [[jnp.where(low,
                            jnp.dot(sr[c][2 * p], v[c][:, p * LANES:(p + 1) * LANES],
                                    preferred_element_type=F32),
                            jnp.dot(sr[c][2 * p + 1], v[c][:, p * LANES:(p + 1) * LANES],
                                    preferred_element_type=F32))
                  for p in range(RET_HEADS // 2)] for c in range(per_block)]
        states = [jnp.zeros((LANES, RET_WIDTH), F32) if i == 0 else state_s[...]]
        for c in range(per_block):
            states.append(states[c] * cd + jnp.where(same_head, kv[c], 0.0))
        state_s[...] = states[per_block]
        cross = [jnp.dot((q[c] * qd_ref[...]).astype(BF16), states[c].astype(BF16),
                         preferred_element_type=F32) for c in range(per_block)]
        for j in range(split, i + 1):
            values(j)

        for c in range(per_block):
            for p in range(RET_HEADS // 2):
                cols = slice(p * LANES, (p + 1) * LANES)
                o = inner[c][p] + cross[c][:, cols]
                s_lo = jnp.sum(jnp.where(low, o, 0.0), axis=-1, keepdims=True)
                s_hi = jnp.sum(jnp.where(low, 0.0, o), axis=-1, keepdims=True)
                d = o - jnp.where(low, s_lo, s_hi) * (1.0 / RET_DV)
                d2 = d * d
                v_lo = jnp.sum(jnp.where(low, d2, 0.0), axis=-1, keepdims=True)
                v_hi = jnp.sum(jnp.where(low, 0.0, d2), axis=-1, keepdims=True)
                var = jnp.where(low, v_lo, v_hi) * (1.0 / RET_DV)
                y = d * lax.rsqrt(var + NORM_EPS) * gn[:, cols]
                gp = z_ref[rows[c], 2 * LANES + RET_WIDTH + p * LANES:
                           2 * LANES + RET_WIDTH + (p + 1) * LANES]
                yb_ref[rows[c], cols] = (gp * jax.nn.sigmoid(gp) * y).astype(BF16)

        for p in range(MLA_HEADS // 2):
            even = acc_s[2 * p]
            odd = acc_s[2 * p + 1]
            pair = jnp.where(value_rows, even / even[MLA_V:MLA_V + 1], odd / odd[0:1])
            ya_ref[i * tq:(i + 1) * tq, p * LANES:(p + 1) * LANES] = pair.T.astype(BF16)

    def body(i, carry):
        lax.switch(i, [functools.partial(block, n) for n in range(nq)])
        return carry

    lax.fori_loop(0, nq, body, 0)


def _mixers(qt, k, vt, zr, cos, sin, consts, gn, l, tq):
    B, S, _ = k.shape
    nq = S // tq
    decay, qd, kd, cd = consts
    tr_spec = pl.BlockSpec((None, nq, MLA_PAD, tq), lambda b: (b, 0, 0, 0))
    row = lambda width: pl.BlockSpec((None, S, width), lambda b: (b, 0, 0))
    full = lambda a: pl.BlockSpec(a.shape, lambda b: (0,) * a.ndim)
    return pl.pallas_call(
        functools.partial(_mixers_kernel, tq=tq, nq=nq),
        grid=(B,),
        in_specs=[tr_spec, row(MLA_PAD), tr_spec, row(Z_RET), row(LANES), row(LANES),
                  full(decay), full(qd), full(kd), full(cd),
                  pl.BlockSpec((None, 1, RET_WIDTH), lambda b: (l, 0, 0))],
        out_specs=[row(MLA_WIDTH), row(RET_WIDTH)],
        out_shape=[jax.ShapeDtypeStruct((B, S, MLA_WIDTH), BF16),
                   jax.ShapeDtypeStruct((B, S, RET_WIDTH), BF16)],
        scratch_shapes=[pltpu.VMEM((MLA_HEADS, nq, tq, tq), F32),
                        pltpu.VMEM((MLA_HEADS, 8, tq), F32),
                        pltpu.VMEM((MLA_HEADS, HEAD_PAD, tq), F32),
                        pltpu.VMEM((LANES, RET_WIDTH), F32)],
        compiler_params=_params("parallel"),
        name="mixers",
    )(qt, k, vt, zr, cos, sin, decay, qd, kd, cd, gn)


def _out_ffn_kernel(x_ref, ya_ref, yb_ref, yc_ref, mod_ref, g_ref, wo_ref, wg_ref, wu_ref,
                    wd_ref, fmod_ref, fg_ref, o_ref, h_s, acc_s, *, final, halves):
    B, tt, D = x_ref.shape
    th = tt // halves
    rows = B * th
    mod = mod_ref[...]
    a0, b0 = MLA_WIDTH, MLA_WIDTH + RET_WIDTH
    n_chunks = FFN_HIDDEN // FFN_CHUNK

    def out_proj(k):
        tok = slice(k * th, (k + 1) * th)
        r = slice(k * rows, (k + 1) * rows)
        yc = pltpu.einshape("tbd->btd", yc_ref[tok]).reshape(rows, LRU_WIDTH).astype(BF16)
        y = (jnp.dot(ya_ref[:, tok, :].reshape(rows, MLA_WIDTH), wo_ref[0:a0, :],
                     preferred_element_type=F32)
             + jnp.dot(yb_ref[:, tok, :].reshape(rows, RET_WIDTH), wo_ref[a0:b0, :],
                       preferred_element_type=F32)
             + jnp.dot(yc, wo_ref[b0:, :], preferred_element_type=F32))
        x1 = x_ref[:, tok, :] + mod[:, 2:3] * y.reshape(B, th, D)
        o_ref[:, tok, :] = x1
        h = _rms(x1, g_ref[...]) * (1.0 + mod[:, 4:5]) + mod[:, 3:4]
        h_s[r, :] = h.reshape(rows, D).astype(BF16)

    def ffn(k):
        tok = slice(k * th, (k + 1) * th)
        r = slice(k * rows, (k + 1) * rows)

        def gate_up(j):
            cols = slice(j * FFN_CHUNK, (j + 1) * FFN_CHUNK)
            h = h_s[r, :]
            return (jnp.dot(h, wg_ref[:, cols], preferred_element_type=F32),
                    jnp.dot(h, wu_ref[:, cols], preferred_element_type=F32))

        nxt = gate_up(0)
        for j in range(n_chunks):
            gate, up = nxt
            if j + 1 < n_chunks:
                nxt = gate_up(j + 1)
            act = (gate * jax.nn.sigmoid(gate) * up).astype(BF16)
            down = jnp.dot(act, wd_ref[j * FFN_CHUNK:(j + 1) * FFN_CHUNK, :],
                           preferred_element_type=F32)
            acc_s[r, :] = down if j == 0 else acc_s[r, :] + down
        x2 = o_ref[:, tok, :] + mod[:, 5:6] * acc_s[r, :].reshape(B, th, D)
        if final:
            fmod = fmod_ref[...]
            x2 = _rms(x2, fg_ref[...]) * (1.0 + fmod[:, 1:2]) + fmod[:, 0:1]
        o_ref[:, tok, :] = x2

    for k in range(halves):
        out_proj(k)
    for k in range(halves):
        ffn(k)


def _out_ffn(x, ya, yb, yc, mod, norm, wo, wgu, wd, fmod, fnorm, l, tt, halves, final):
    B, S, D = x.shape
    tok = lambda width: pl.BlockSpec((B, tt, width), lambda i: (0, i, 0))
    return pl.pallas_call(
        functools.partial(_out_ffn_kernel, final=final, halves=halves),
        grid=(S // tt,),
        in_specs=[tok(D), tok(MLA_WIDTH), tok(RET_WIDTH),
                  pl.BlockSpec((tt, B, LRU_WIDTH), lambda i: (i, 0, 0)),
                  pl.BlockSpec((None, B, N_MOD, D), lambda i: (l, 0, 0, 0)),
                  pl.BlockSpec((None, 1, D), lambda i: (l, 0, 0)),
                  _resident((None, D, D), lambda i: (l, 0, 0)),
                  _resident((None, D, FFN_HIDDEN), lambda i: (l, 0, 0)),
                  _resident((None, D, FFN_HIDDEN), lambda i: (l, 0, 1)),
                  _resident((None, FFN_HIDDEN, D), lambda i: (l, 0, 0)),
                  pl.BlockSpec((B, 2, D), lambda i: (0, 0, 0)),
                  pl.BlockSpec((1, D), lambda i: (0, 0))],
        out_specs=tok(D),
        out_shape=jax.ShapeDtypeStruct((B, S, D), F32),
        scratch_shapes=[pltpu.VMEM((B * tt, D), BF16), pltpu.VMEM((B * tt, D), F32)],
        compiler_params=_params("parallel"),
        name="out_ffn",
    )(x, ya, yb, yc, mod, norm, wo, wgu, wgu, wd, fmod, fnorm)


def _split_in_proj(w_in):
    n_mla = MLA_Q_RANK + MLA_KV_RANK + MLA_ROPE
    w_mla = jnp.pad(w_in[..., :n_mla].astype(BF16), ((0, 0), (0, 0), (0, Z_MLA - n_mla)))
    return (w_mla, w_in[..., n_mla:n_mla + Z_RET].astype(BF16),
            w_in[..., n_mla + Z_RET:].astype(BF16))


def _pad_heads(w, width):
    L, R, H, _ = w.shape
    return jnp.pad(w, ((0, 0), (0, 0), (0, 0), (0, HEAD_PAD - width))).reshape(L, R, H * HEAD_PAD)


def _mla_weights(w_uq, w_ukv):
    L = w_uq.shape[0]
    wq = _pad_heads(w_uq.reshape(L, MLA_Q_RANK, MLA_HEADS, MLA_NOPE + MLA_ROPE), MLA_NOPE + MLA_ROPE)
    kv = w_ukv.reshape(L, MLA_KV_RANK, MLA_HEADS, MLA_NOPE + MLA_V)
    wk = _pad_heads(kv[..., :MLA_NOPE], MLA_NOPE)
    v = kv[..., MLA_NOPE:]
    odd = (jnp.arange(MLA_HEADS) % 2 == 1)[None, None, :, None]
    zero = jnp.zeros_like(v)
    wv = jnp.concatenate([jnp.where(odd, zero, v), jnp.where(odd, v, zero)], axis=-1)
    wv = wv.reshape(L, MLA_KV_RANK, MLA_PAD)
    r = jnp.arange(LANES)[:, None]
    c = jnp.arange(MLA_PAD)[None, :]
    wr = ((r < MLA_ROPE) & (c % HEAD_PAD == MLA_NOPE + r)).astype(BF16)
    ones = ((c % HEAD_PAD) == jnp.where((c // HEAD_PAD) % 2 == 0, MLA_V, 0)).astype(F32)
    wqt = jnp.swapaxes(wq, 1, 2).astype(BF16)
    wvt = jnp.swapaxes(wv, 1, 2).astype(BF16)
    return wqt, wk.astype(BF16), wr, wvt, ones.reshape(MLA_PAD, 1)


def _block_diag(w):
    L, G, I, J = w.shape
    eye = jnp.eye(G, dtype=w.dtype)
    return jnp.einsum('lgij,gh->lgihj', w, eye).reshape(L, G * I, G * J)


def kernel(x, c, positions, mod_w, mod_b, norm1, w_in, mla_q_norm, mla_w_uq, mla_kv_norm, mla_w_ukv, ret_gn, lru_conv_w, lru_conv_b, lru_w_a, lru_b_a, lru_w_i, lru_b_i, lru_lambda, w_out, norm2, w_gate_up, w_down, final_norm, final_mod_w, final_mod_b):
    B, S, D = x.shape
    L = mod_w.shape[0]
    tm = min(S, 512)
    halves = 2
    tt = halves * tm // B
    tq = min(S, 256)
    row = lambda a: a.reshape(L, 1, a.shape[-1])

    mod = _modulation(c, mod_w, mod_b, 1536).reshape(L, B, N_MOD, D)
    fmod = _modulation(c, final_mod_w[None], final_mod_b[None], 1024).reshape(B, 2, D)
    fnorm = final_norm.reshape(1, D)
    tables = _rope_tables(positions)

    w_in_groups = _split_in_proj(w_in)
    wqt, wk, wr, wvt, ones = _mla_weights(mla_w_uq, mla_w_ukv)
    wa = _block_diag(lru_w_a).astype(BF16)
    wi = _block_diag(lru_w_i).astype(BF16)
    wo = w_out.astype(BF16)
    wgu = w_gate_up.astype(BF16)
    wd = w_down.astype(BF16)
    ret_consts = _retention_consts()

    for l in range(L):
        zm, zr, yc = _in_proj_lru(x, mod, row(norm1), w_in_groups, lru_conv_w, row(lru_conv_b), wa,
                                  row(lru_b_a), wi, row(lru_b_i), row(lru_lambda), l, tt, halves)
        qt, k, vt = _mla_prep(zm, tables, mla_q_norm, mla_kv_norm, wqt, wk, wr, wvt, ones,
                              l, min(S, 2 * tm), tq)
        ya, yb = _mixers(qt, k, vt, zr, tables[0], tables[1], ret_consts, row(ret_gn), l, tq)
        x = _out_ffn(x, ya, yb, yc, mod, row(norm2), wo, wgu, wd, fmod, fnorm, l, tt, halves,
                     final=(l == L - 1))
    return x
```

```python
import functools

import jax
import jax.numpy as jnp
from jax import lax
from jax.experimental import pallas as pl
from jax.experimental.pallas import tpu as pltpu

D_MODEL = 1024
MLA_HEADS = 6
MLA_Q_RANK = 256
MLA_KV_RANK = 128
MLA_NOPE = 64
MLA_ROPE = 32
MLA_V = 64
MLA_WIDTH = MLA_HEADS * MLA_V
RET_HEADS = 4
RET_DK = 32
RET_DV = 64
RET_WIDTH = RET_HEADS * RET_DV
RET_CHUNK = 128
LRU_WIDTH = D_MODEL - MLA_WIDTH - RET_WIDTH
LRU_BLOCKS = 6
LRU_BLOCK = LRU_WIDTH // LRU_BLOCKS
CONV_WIDTH = 4
LRU_C = 8.0
FFN_HIDDEN = 2816
ROPE_BASE = 10000.0
NORM_EPS = 1e-6
LOG2_E = 1.4426950408889634
N_MOD = 6

LANES = 128
HEAD_PAD = LANES
MLA_PAD = MLA_HEADS * HEAD_PAD
Z_MLA = 512
Z_RET = 2 * RET_HEADS * RET_DK + 2 * RET_WIDTH
Z_LRU = 2 * LRU_WIDTH
FFN_CHUNK = 256
VMEM_LIMIT = 56 * 1024 * 1024

BF16 = jnp.bfloat16
F32 = jnp.float32


def _params(*sem):
    return pltpu.CompilerParams(dimension_semantics=sem, vmem_limit_bytes=VMEM_LIMIT)


def _resident(shape, index_map):
    return pl.BlockSpec(shape, index_map, pipeline_mode=pl.Buffered(1))


def _rms(x, g):
    return x * lax.rsqrt(jnp.mean(x * x, axis=-1, keepdims=True) + NORM_EPS) * g


def _swap_halves(x, first_half):
    return jnp.where(first_half, pltpu.roll(x, LANES - 16, 1), pltpu.roll(x, 16, 1))


def _mod_kernel(c_ref, w_ref, b_ref, o_ref):
    c = c_ref[...]
    ca = (c * jax.nn.sigmoid(c)).astype(BF16)
    o_ref[...] = jnp.dot(ca, w_ref[...].astype(BF16), preferred_element_type=F32) + b_ref[...]


def _modulation(c, w, b, tn):
    L, D, N = w.shape
    B = c.shape[0]
    return pl.pallas_call(
        _mod_kernel,
        grid=(L, N // tn),
        in_specs=[pl.BlockSpec((B, D), lambda l, j: (0, 0)),
                  pl.BlockSpec((None, D, tn), lambda l, j: (l, 0, j)),
                  pl.BlockSpec((None, 1, tn), lambda l, j: (l, 0, j))],
        out_specs=pl.BlockSpec((None, B, tn), lambda l, j: (l, 0, j)),
        out_shape=jax.ShapeDtypeStruct((L, B, N), F32),
        compiler_params=_params("parallel", "parallel"),
        name="modulation",
    )(c, w, b.reshape(L, 1, N))


def _rope_table_kernel(pos_ref, inv_ref, cos_ref, sin_ref, cost_ref, sint_ref):
    ang = inv_ref[...] * pos_ref[...].astype(F32)
    cos = jnp.cos(ang)
    sin = jnp.sin(ang)
    cost_ref[...] = cos
    sint_ref[...] = sin
    reps = LANES // (2 * cos.shape[0])
    cos_ref[...] = jnp.concatenate([cos] * (2 * reps), axis=0).T
    sin_ref[...] = jnp.concatenate([-sin, sin] * reps, axis=0).T


def _rope_tables(positions):
    B, S = positions.shape
    half = MLA_ROPE // 2
    inv = ROPE_BASE ** (-jnp.arange(half, dtype=F32) / half)
    ts = min(S, 1024)
    spec = pl.BlockSpec((None, ts, LANES), lambda b, i: (b, i, 0))
    spec_t = pl.BlockSpec((None, half, ts), lambda b, i: (b, 0, i))
    return pl.pallas_call(
        _rope_table_kernel,
        grid=(B, S // ts),
        in_specs=[pl.BlockSpec((None, 1, ts), lambda b, i: (b, 0, i)),
                  pl.BlockSpec((half, 1), lambda b, i: (0, 0))],
        out_specs=[spec, spec, spec_t, spec_t],
        out_shape=[jax.ShapeDtypeStruct((B, S, LANES), F32)] * 2
        + [jax.ShapeDtypeStruct((B, half, S), F32)] * 2,
        compiler_params=_params("parallel", "parallel"),
        name="rope_tables",
    )(positions.reshape(B, 1, S), inv.reshape(half, 1))


def _softplus(x):
    return jnp.maximum(x, 0.0) + jnp.log1p(jnp.exp(-jnp.abs(x)))


def _gelu_tanh(x):
    return 0.5 * x * (1.0 + jnp.tanh(0.7978845608028654 * (x + 0.044715 * (x * x * x))))


def _sigmoid(x):
    return 0.5 * jnp.tanh(0.5 * x) + 0.5


def _in_proj_lru_kernel(x_ref, mod_ref, g_ref, wm_ref, wr_ref, wl_ref, cw_ref, cb_ref, wa_ref,
                        ba_ref, wi_ref, bi_ref, lam_ref, zm_ref, zr_ref, yc_ref, xbuf, h_s,
                        *, halves):
    B, tt, D = x_ref.shape
    th = tt // halves
    W = LRU_WIDTH
    PAD = 8
    taps = CONV_WIDTH - 1

    @pl.when(pl.program_id(0) == 0)
    def _():
        xbuf[0:PAD] = jnp.zeros((PAD, B, W), F32)
        h_s[...] = jnp.zeros((B, W), F32)

    mod = mod_ref[...]
    cw = cw_ref[...]
    log_a_rate = -LRU_C * _softplus(-lam_ref[...])

    def half(k):
        tok = slice(k * th, (k + 1) * th)
        h = _rms(x_ref[:, tok, :], g_ref[...]) * (1.0 + mod[:, 1:2]) + mod[:, 0:1]
        h = h.reshape(B * th, D).astype(BF16)

        zl = jnp.dot(h, wl_ref[...], preferred_element_type=F32)
        zm_ref[:, tok, :] = jnp.dot(h, wm_ref[...],
                                    preferred_element_type=F32).reshape(B, th, Z_MLA)
        zl = pltpu.einshape("btd->tbd", zl.reshape(B, th, Z_LRU))
        gate = _gelu_tanh(zl[:, :, W:])

        xbuf[PAD:PAD + th] = zl[:, :, :W]
        xc = cb_ref[...].reshape(1, 1, W) + sum(
            xbuf[PAD - taps + j:PAD - taps + j + th] * cw[j].reshape(1, 1, W)
            for j in range(CONV_WIDTH))
        xbuf[PAD - taps:PAD] = xbuf[PAD + th - taps:PAD + th]

        xc2 = xc.reshape(th * B, W)
        xb = xc2.astype(BF16)
        r = _sigmoid(jnp.dot(xb, wa_ref[...], preferred_element_type=F32) + ba_ref[...])
        i = _sigmoid(jnp.dot(xb, wi_ref[...], preferred_element_type=F32) + bi_ref[...])
        zr_ref[:, tok, :] = jnp.dot(h, wr_ref[...],
                                    preferred_element_type=F32).reshape(B, th, Z_RET)

        log_a = log_a_rate * r
        a = jnp.exp(log_a)
        one_minus_a2 = -jnp.tanh(log_a) * (a * a + 1.0)
        root = jnp.where(one_minus_a2 > 0.0, one_minus_a2 * lax.rsqrt(one_minus_a2), 0.0)
        a = a.reshape(th, B, W)
        b = (root * i * xc2).reshape(th, B, W)

        hid = h_s[...]
        for t in range(th):
            hid = a[t] * hid + b[t]
            yc_ref[k * th + t] = gate[t] * hid
        h_s[...] = hid

    for k in range(halves):
        half(k)


def _in_proj_lru(x, mod, norm, w_groups, cw, cb, wa, ba, wi, bi, lam, l, tt, halves):
    B, S, D = x.shape
    W = LRU_WIDTH
    tok = lambda width: pl.BlockSpec((B, tt, width), lambda i: (0, i, 0))
    vec = pl.BlockSpec((None, 1, W), lambda i: (l, 0, 0))
    mat = _resident((None, W, W), lambda i: (l, 0, 0))
    return pl.pallas_call(
        functools.partial(_in_proj_lru_kernel, halves=halves),
        grid=(S // tt,),
        in_specs=[tok(D),
                  pl.BlockSpec((None, B, N_MOD, D), lambda i: (l, 0, 0, 0)),
                  pl.BlockSpec((None, 1, D), lambda i: (l, 0, 0)),
                  *[_resident((None, D, w.shape[-1]), lambda i: (l, 0, 0)) for w in w_groups],
                  pl.BlockSpec((None, CONV_WIDTH, W), lambda i: (l, 0, 0)),
                  vec, mat, vec, mat, vec, vec],
        out_specs=[tok(Z_MLA), tok(Z_RET),
                   pl.BlockSpec((tt, B, W), lambda i: (i, 0, 0))],
        out_shape=[jax.ShapeDtypeStruct((B, S, Z_MLA), F32),
                   jax.ShapeDtypeStruct((B, S, Z_RET), F32),
                   jax.ShapeDtypeStruct((S, B, W), F32)],
        scratch_shapes=[pltpu.VMEM((tt // halves + 8, B, W), F32), pltpu.VMEM((B, W), F32)],
        compiler_params=_params("arbitrary"),
        name="in_proj_lru",
    )(x, mod, norm, *w_groups, cw, cb, wa, ba, wi, bi, lam)


def _rms_feature_major(x, g):
    return x * lax.rsqrt(jnp.mean(x * x, axis=0, keepdims=True) + NORM_EPS) * g


def _mla_prep_kernel(z_ref, cos_ref, sin_ref, cost_ref, sint_ref, qn_ref, kvn_ref, kvnr_ref,
                     wqt_ref, wk_ref, wr_ref, wvt_ref, one_ref, qt_ref, k_ref, vt_ref, *, tq):
    z = z_ref[...]
    cos = cos_ref[...]
    sin = sin_ref[...]
    lane = lax.broadcasted_iota(jnp.int32, cos.shape, 1)
    first_half = (lane % 32) < 16

    c_kv = _rms(z[:, MLA_Q_RANK:MLA_Q_RANK + MLA_KV_RANK], kvnr_ref[...]).astype(BF16)
    kr = z[:, MLA_Q_RANK + MLA_KV_RANK:]
    kr = jnp.where(lane < MLA_ROPE, kr * cos + _swap_halves(kr, first_half) * sin, 0.0)
    k = (jnp.dot(c_kv, wk_ref[...], preferred_element_type=F32)
         + jnp.dot(kr.astype(BF16), wr_ref[...], preferred_element_type=F32))
    k_ref[...] = k.astype(BF16)

    zt = z[:, :MLA_Q_RANK + MLA_KV_RANK].T
    c_q = _rms_feature_major(zt[:MLA_Q_RANK], qn_ref[...]).astype(BF16)
    q = jnp.dot(wqt_ref[...], c_q, preferred_element_type=F32)
    qk_scale = (MLA_NOPE + MLA_ROPE) ** -0.5 * LOG2_E
    cos_t = cost_ref[...] * qk_scale
    sin_t = sint_ref[...] * qk_scale
    half = MLA_ROPE // 2
    rows = []
    for h in range(MLA_HEADS):
        r0 = h * HEAD_PAD + MLA_NOPE
        x1 = q[r0:r0 + half]
        x2 = q[r0 + half:r0 + MLA_ROPE]
        rows += [q[h * HEAD_PAD:r0] * qk_scale, x1 * cos_t - x2 * sin_t, x2 * cos_t + x1 * sin_t,
                 q[r0 + MLA_ROPE:(h + 1) * HEAD_PAD]]
    q = jnp.concatenate(rows, axis=0).astype(BF16)
    c_kv_t = _rms_feature_major(zt[MLA_Q_RANK:], kvn_ref[...]).astype(BF16)
    v = (jnp.dot(wvt_ref[...], c_kv_t, preferred_element_type=F32) + one_ref[...]).astype(BF16)
    for t in range(z.shape[0] // tq):
        qt_ref[t] = q[:, t * tq:(t + 1) * tq]
        vt_ref[t] = v[:, t * tq:(t + 1) * tq]


def _mla_prep(zm, tables, qn, kvn, wqt, wk, wr, wvt, ones, l, tm, tq):
    B, S, _ = zm.shape
    cos, sin, cos_t, sin_t = tables
    tok = lambda w: pl.BlockSpec((None, tm, w), lambda b, i: (b, i, 0))
    tab_t = pl.BlockSpec((None, MLA_ROPE // 2, tm), lambda b, i: (b, 0, i))
    col = lambda n: pl.BlockSpec((None, n, 1), lambda b, i: (l, 0, 0))
    tr_spec = pl.BlockSpec((None, tm // tq, MLA_PAD, tq), lambda b, i: (b, i, 0, 0))
    tr_shape = jax.ShapeDtypeStruct((B, S // tq, MLA_PAD, tq), BF16)
    L = qn.shape[0]
    return pl.pallas_call(
        functools.partial(_mla_prep_kernel, tq=tq),
        grid=(B, S // tm),
        in_specs=[tok(Z_MLA), tok(LANES), tok(LANES), tab_t, tab_t,
                  col(MLA_Q_RANK), col(MLA_KV_RANK),
                  pl.BlockSpec((None, 1, MLA_KV_RANK), lambda b, i: (l, 0, 0)),
                  _resident((None, MLA_PAD, MLA_Q_RANK), lambda b, i: (l, 0, 0)),
                  _resident((None, MLA_KV_RANK, MLA_PAD), lambda b, i: (l, 0, 0)),
                  _resident((LANES, MLA_PAD), lambda b, i: (0, 0)),
                  _resident((None, MLA_PAD, MLA_KV_RANK), lambda b, i: (l, 0, 0)),
                  pl.BlockSpec((MLA_PAD, 1), lambda b, i: (0, 0))],
        out_specs=[tr_spec, tok(MLA_PAD), tr_spec],
        out_shape=[tr_shape, jax.ShapeDtypeStruct((B, S, MLA_PAD), BF16), tr_shape],
        compiler_params=_params("parallel", "parallel"),
        name="mla_prep",
    )(zm, cos, sin, cos_t, sin_t, qn.reshape(L, -1, 1), kvn.reshape(L, -1, 1),
      kvn.reshape(L, 1, -1), wqt, wk, wr, wvt, ones)


def _retention_consts():
    C, H = RET_CHUNK, RET_HEADS
    log_g = jnp.log(1.0 - jnp.exp2(-5.0 - jnp.arange(H, dtype=F32)))
    idx = jnp.arange(C, dtype=F32)
    diff = idx[:, None] - idx[None, :]
    decay = jnp.where(diff >= 0, jnp.exp(log_g[:, None, None] * jnp.maximum(diff, 0.0)), 0.0)
    q_decay = jnp.exp(log_g[:, None] * (idx + 1.0))
    k_decay = jnp.exp(log_g[:, None] * (C - 1.0 - idx))
    chunk_decay = jnp.exp(log_g * C)
    qd = jnp.repeat(q_decay.T, RET_DK, axis=1)
    kd = jnp.repeat(k_decay.T, RET_DK, axis=1)
    cd = jnp.broadcast_to(jnp.repeat(chunk_decay, RET_DK)[:, None], (H * RET_DK, RET_WIDTH))
    return decay, qd, kd, cd


def _mixers_kernel(qt_ref, k_ref, vt_ref, z_ref, cos_ref, sin_ref, dec_ref, qd_ref, kd_ref,
                   cd_ref, gn_ref, ya_ref, yb_ref, s_s, state_s, *, tq, nq):
    groups = tq // 8
    key = lax.broadcasted_iota(jnp.int32, (tq, tq), 0)
    qry = lax.broadcasted_iota(jnp.int32, (tq, tq), 1)
    causal = key <= qry
    value_rows = lax.broadcasted_iota(jnp.int32, (HEAD_PAD, tq), 0) < MLA_V
    hs = lambda h: slice(h * HEAD_PAD, (h + 1) * HEAD_PAD)

    C = RET_CHUNK
    per_block = tq // C
    lane = lax.broadcasted_iota(jnp.int32, (C, LANES), 1)
    first_half = (lane % 32) < 16
    low = lane < RET_DV
    head_of_lane = lane // RET_DK
    srow = lax.broadcasted_iota(jnp.int32, (LANES, RET_WIDTH), 0) // RET_DK
    scol = lax.broadcasted_iota(jnp.int32, (LANES, RET_WIDTH), 1) // RET_DV
    same_head = srow == scol
    gn = gn_ref[...]
    cd = cd_ref[...]

    def rotate(x, rows):
        return x * cos_ref[rows, :] + _swap_halves(x, first_half) * sin_ref[rows, :]

    def block(i):
        rows = [slice((i * per_block + c) * C, (i * per_block + c + 1) * C)
                for c in range(per_block)]
        q = [rotate(z_ref[r, 0:LANES], r) for r in rows]
        k = [rotate(z_ref[r, LANES:2 * LANES], r) * (RET_DK ** -0.5) for r in rows]
        kb = [x.astype(BF16) for x in k]
        v = [z_ref[r, 2 * LANES:2 * LANES + RET_WIDTH].astype(BF16) for r in rows]

        def scores(h):
            m = None
            for j in range(i + 1):
                s = jnp.dot(k_ref[j * tq:(j + 1) * tq, hs(h)], qt_ref[i, hs(h), :],
                            preferred_element_type=F32)
                if j == i:
                    s = jnp.where(causal, s, -jnp.inf)
                s_s[h, j] = s
                mj = jnp.max(s.reshape(groups, 8, tq), axis=0)
                m = mj if m is None else jnp.maximum(m, mj)
            return jnp.broadcast_to(jnp.max(m, axis=0, keepdims=True), (8, tq))

        def values(h, m):
            acc = None
            for j in range(i + 1):
                p = jnp.exp2(s_s[h, j].reshape(groups, 8, tq) - m[None])
                pv = jnp.dot(vt_ref[j, hs(h), :], p.reshape(tq, tq).astype(BF16),
                             preferred_element_type=F32)
                acc = pv if acc is None else acc + pv
            return acc

        m_next = scores(0)
        sr = [[(lax.dot_general(jnp.where(head_of_lane == h, q[c], 0.0).astype(BF16), kb[c],
                                (((1,), (1,)), ((), ())), preferred_element_type=F32)
                * dec_ref[h]).astype(BF16) for h in range(RET_HEADS)] for c in range(per_block)]
        kv = [lax.dot_general((k[c] * kd_ref[...]).astype(BF16), v[c], (((0,), (0,)), ((), ())),
                              preferred_element_type=F32) for c in range(per_block)]
        acc = [None] * MLA_HEADS
        for h in range(MLA_HEADS):
            m = m_next
            if h + 1 < MLA_HEADS:
                m_next = scores(h + 1)
            acc[h] = values(h, m)
            if h == MLA_HEADS // 2 - 1:
                inner = [[jnp.where(low,
                                    jnp.dot(sr[c][2 * p], v[c][:, p * LANES:(p + 1) * LANES],
                                            preferred_element_type=F32),
                                    jnp.dot(sr[c][2 * p + 1], v[c][:, p * LANES:(p + 1) * LANES],
                                            preferred_element_type=F32))
                          for p in range(RET_HEADS // 2)] for c in range(per_block)]
                states = [jnp.zeros((LANES, RET_WIDTH), F32) if i == 0 else state_s[...]]
                for c in range(per_block):
                    states.append(states[c] * cd + jnp.where(same_head, kv[c], 0.0))
                state_s[...] = states[per_block]
                cross = [jnp.dot((q[c] * qd_ref[...]).astype(BF16), states[c].astype(BF16),
                                 preferred_element_type=F32) for c in range(per_block)]
            if h % 2 == 1:
                even, odd = acc[h - 1], acc[h]
                pair = jnp.where(value_rows, even / even[MLA_V:MLA_V + 1], odd / odd[0:1])
                ya_ref[i * tq:(i + 1) * tq, (h // 2) * LANES:(h // 2 + 1) * LANES] = (
                    pair.T.astype(BF16))

        for c in range(per_block):
            for p in range(RET_HEADS // 2):
                cols = slice(p * LANES, (p + 1) * LANES)
                o = inner[c][p] + cross[c][:, cols]
                s_lo = jnp.sum(jnp.where(low, o, 0.0), axis=-1, keepdims=True)
                s_hi = jnp.sum(jnp.where(low, 0.0, o), axis=-1, keepdims=True)
                d = o - jnp.where(low, s_lo, s_hi) * (1.0 / RET_DV)
                d2 = d * d
                v_lo = jnp.sum(jnp.where(low, d2, 0.0), axis=-1, keepdims=True)
                v_hi = jnp.sum(jnp.where(low, 0.0, d2), axis=-1, keepdims=True)
                var = jnp.where(low, v_lo, v_hi) * (1.0 / RET_DV)
                y = d * lax.rsqrt(var + NORM_EPS) * gn[:, cols]
                gp = z_ref[rows[c], 2 * LANES + RET_WIDTH + p * LANES:
                           2 * LANES + RET_WIDTH + (p + 1) * LANES]
                yb_ref[rows[c], cols] = (gp * jax.nn.sigmoid(gp) * y).astype(BF16)

    def body(i, carry):
        lax.switch(i, [functools.partial(block, n) for n in range(nq)])
        return carry

    lax.fori_loop(0, nq, body, 0)


def _mixers(qt, k, vt, zr, cos, sin, consts, gn, l, tq):
    B, S, _ = k.shape
    nq = S // tq
    decay, qd, kd, cd = consts
    tr_spec = pl.BlockSpec((None, nq, MLA_PAD, tq), lambda b: (b, 0, 0, 0))
    row = lambda width: pl.BlockSpec((None, S, width), lambda b: (b, 0, 0))
    full = lambda a: pl.BlockSpec(a.shape, lambda b: (0,) * a.ndim)
    return pl.pallas_call(
        functools.partial(_mixers_kernel, tq=tq, nq=nq),
        grid=(B,),
        in_specs=[tr_spec, row(MLA_PAD), tr_spec, row(Z_RET), row(LANES), row(LANES),
                  full(decay), full(qd), full(kd), full(cd),
                  pl.BlockSpec((None, 1, RET_WIDTH), lambda b: (l, 0, 0))],
        out_specs=[row(MLA_WIDTH), row(RET_WIDTH)],
        out_shape=[jax.ShapeDtypeStruct((B, S, MLA_WIDTH), BF16),
                   jax.ShapeDtypeStruct((B, S, RET_WIDTH), BF16)],
        scratch_shapes=[pltpu.VMEM((MLA_HEADS, nq, tq, tq), F32),
                        pltpu.VMEM((LANES, RET_WIDTH), F32)],
        compiler_params=_params("parallel"),
        name="mixers",
    )(qt, k, vt, zr, cos, sin, decay, qd, kd, cd, gn)


def _out_ffn_kernel(x_ref, ya_ref, yb_ref, yc_ref, mod_ref, g_ref, wo_ref, wg_ref, wu_ref,
                    wd_ref, fmod_ref, fg_ref, o_ref, h_s, acc_s, *, final, halves):
    B, tt, D = x_ref.shape
    th = tt // halves
    rows = B * th
    mod = mod_ref[...]
    a0, b0 = MLA_WIDTH, MLA_WIDTH + RET_WIDTH
    n_chunks = FFN_HIDDEN // FFN_CHUNK

    def out_proj(k):
        tok = slice(k * th, (k + 1) * th)
        r = slice(k * rows, (k + 1) * rows)
        yc = pltpu.einshape("tbd->btd", yc_ref[tok]).reshape(rows, LRU_WIDTH).astype(BF16)
        y = (jnp.dot(ya_ref[:, tok, :].reshape(rows, MLA_WIDTH), wo_ref[0:a0, :],
                     preferred_element_type=F32)
             + jnp.dot(yb_ref[:, tok, :].reshape(rows, RET_WIDTH), wo_ref[a0:b0, :],
                       preferred_element_type=F32)
             + jnp.dot(yc, wo_ref[b0:, :], preferred_element_type=F32))
        x1 = x_ref[:, tok, :] + mod[:, 2:3] * y.reshape(B, th, D)
        o_ref[:, tok, :] = x1
        h = _rms(x1, g_ref[...]) * (1.0 + mod[:, 4:5]) + mod[:, 3:4]
        h_s[r, :] = h.reshape(rows, D).astype(BF16)

    def ffn(k):
        tok = slice(k * th, (k + 1) * th)
        r = slice(k * rows, (k + 1) * rows)

        def gate_up(j):
            cols = slice(j * FFN_CHUNK, (j + 1) * FFN_CHUNK)
            h = h_s[r, :]
            return (jnp.dot(h, wg_ref[:, cols], preferred_element_type=F32),
                    jnp.dot(h, wu_ref[:, cols], preferred_element_type=F32))

        nxt = gate_up(0)
        for j in range(n_chunks):
            gate, up = nxt
            if j + 1 < n_chunks:
                nxt = gate_up(j + 1)
            act = (gate * jax.nn.sigmoid(gate) * up).astype(BF16)
            down = jnp.dot(act, wd_ref[j * FFN_CHUNK:(j + 1) * FFN_CHUNK, :],
                           preferred_element_type=F32)
            acc_s[r, :] = down if j == 0 else acc_s[r, :] + down
        x2 = o_ref[:, tok, :] + mod[:, 5:6] * acc_s[r, :].reshape(B, th, D)
        if final:
            fmod = fmod_ref[...]
            x2 = _rms(x2, fg_ref[...]) * (1.0 + fmod[:, 1:2]) + fmod[:, 0:1]
        o_ref[:, tok, :] = x2

    for k in range(halves):
        out_proj(k)
    for k in range(halves):
        ffn(k)


def _out_ffn(x, ya, yb, yc, mod, norm, wo, wgu, wd, fmod, fnorm, l, tt, halves, final):
    B, S, D = x.shape
    tok = lambda width: pl.BlockSpec((B, tt, width), lambda i: (0, i, 0))
    return pl.pallas_call(
        functools.partial(_out_ffn_kernel, final=final, halves=halves),
        grid=(S // tt,),
        in_specs=[tok(D), tok(MLA_WIDTH), tok(RET_WIDTH),
                  pl.BlockSpec((tt, B, LRU_WIDTH), lambda i: (i, 0, 0)),
                  pl.BlockSpec((None, B, N_MOD, D), lambda i: (l, 0, 0, 0)),
                  pl.BlockSpec((None, 1, D), lambda i: (l, 0, 0)),
                  _resident((None, D, D), lambda i: (l, 0, 0)),
                  _resident((None, D, FFN_HIDDEN), lambda i: (l, 0, 0)),
                  _resident((None, D, FFN_HIDDEN), lambda i: (l, 0, 1)),
                  _resident((None, FFN_HIDDEN, D), lambda i: (l, 0, 0)),
                  pl.BlockSpec((B, 2, D), lambda i: (0, 0, 0)),
                  pl.BlockSpec((1, D), lambda i: (0, 0))],
        out_specs=tok(D),
        out_shape=jax.ShapeDtypeStruct((B, S, D), F32),
        scratch_shapes=[pltpu.VMEM((B * tt, D), BF16), pltpu.VMEM((B * tt, D), F32)],
        compiler_params=_params("parallel"),
        name="out_ffn",
    )(x, ya, yb, yc, mod, norm, wo, wgu, wgu, wd, fmod, fnorm)


def _split_in_proj(w_in):
    n_mla = MLA_Q_RANK + MLA_KV_RANK + MLA_ROPE
    w_mla = jnp.pad(w_in[..., :n_mla].astype(BF16), ((0, 0), (0, 0), (0, Z_MLA - n_mla)))
    return (w_mla, w_in[..., n_mla:n_mla + Z_RET].astype(BF16),
            w_in[..., n_mla + Z_RET:].astype(BF16))


def _pad_heads(w, width):
    L, R, H, _ = w.shape
    return jnp.pad(w, ((0, 0), (0, 0), (0, 0), (0, HEAD_PAD - width))).reshape(L, R, H * HEAD_PAD)


def _mla_weights(w_uq, w_ukv):
    L = w_uq.shape[0]
    wq = _pad_heads(w_uq.reshape(L, MLA_Q_RANK, MLA_HEADS, MLA_NOPE + MLA_ROPE), MLA_NOPE + MLA_ROPE)
    kv = w_ukv.reshape(L, MLA_KV_RANK, MLA_HEADS, MLA_NOPE + MLA_V)
    wk = _pad_heads(kv[..., :MLA_NOPE], MLA_NOPE)
    v = kv[..., MLA_NOPE:]
    odd = (jnp.arange(MLA_HEADS) % 2 == 1)[None, None, :, None]
    zero = jnp.zeros_like(v)
    wv = jnp.concatenate([jnp.where(odd, zero, v), jnp.where(odd, v, zero)], axis=-1)
    wv = wv.reshape(L, MLA_KV_RANK, MLA_PAD)
    r = jnp.arange(LANES)[:, None]
    c = jnp.arange(MLA_PAD)[None, :]
    wr = ((r < MLA_ROPE) & (c % HEAD_PAD == MLA_NOPE + r)).astype(BF16)
    ones = ((c % HEAD_PAD) == jnp.where((c // HEAD_PAD) % 2 == 0, MLA_V, 0)).astype(F32)
    wqt = jnp.swapaxes(wq, 1, 2).astype(BF16)
    wvt = jnp.swapaxes(wv, 1, 2).astype(BF16)
    return wqt, wk.astype(BF16), wr, wvt, ones.reshape(MLA_PAD, 1)


def _block_diag(w):
    L, G, I, J = w.shape
    eye = jnp.eye(G, dtype=w.dtype)
    return jnp.einsum('lgij,gh->lgihj', w, eye).reshape(L, G * I, G * J)


def kernel(x, c, positions, mod_w, mod_b, norm1, w_in, mla_q_norm, mla_w_uq, mla_kv_norm, mla_w_ukv, ret_gn, lru_conv_w, lru_conv_b, lru_w_a, lru_b_a, lru_w_i, lru_b_i, lru_lambda, w_out, norm2, w_gate_up, w_down, final_norm, final_mod_w, final_mod_b):
    B, S, D = x.shape
    L = mod_w.shape[0]
    tm = min(S, 512)
    halves = 2
    tt = halves * tm // B
    tq = min(S, 256)
    row = lambda a: a.reshape(L, 1, a.shape[-1])

    mod = _modulation(c, mod_w, mod_b, 1536).reshape(L, B, N_MOD, D)
    fmod = _modulation(c, final_mod_w[None], final_mod_b[None], 1024).reshape(B, 2, D)
    fnorm = final_norm.reshape(1, D)
    tables = _rope_tables(positions)

    w_in_groups = _split_in_proj(w_in)
    wqt, wk, wr, wvt, ones = _mla_weights(mla_w_uq, mla_w_ukv)
    wa = _block_diag(lru_w_a).astype(BF16)
    wi = _block_diag(lru_w_i).astype(BF16)
    wo = w_out.astype(BF16)
    wgu = w_gate_up.astype(BF16)
    wd = w_down.astype(BF16)
    ret_consts = _retention_consts()

    for l in range(L):
        zm, zr, yc = _in_proj_lru(x, mod, row(norm1), w_in_groups, lru_conv_w, row(lru_conv_b), wa,
                                  row(lru_b_a), wi, row(lru_b_i), row(lru_lambda), l, tt, halves)
        qt, k, vt = _mla_prep(zm, tables, mla_q_norm, mla_kv_norm, wqt, wk, wr, wvt, ones,
                              l, min(S, 2 * tm), tq)
        ya, yb = _mixers(qt, k, vt, zr, tables[0], tables[1], ret_consts, row(ret_gn), l, tq)
        x = _out_ffn(x, ya, yb, yc, mod, row(norm2), wo, wgu, wd, fmod, fnorm, l, tt, halves,
                     final=(l == L - 1))
    return x
```

```python
import functools

import jax
import jax.numpy as jnp
from jax import lax
from jax.experimental import pallas as pl
from jax.experimental.pallas import tpu as pltpu

D_MODEL = 1024
MLA_HEADS = 6
MLA_Q_RANK = 256
MLA_KV_RANK = 128
MLA_NOPE = 64
MLA_ROPE = 32
MLA_V = 64
MLA_WIDTH = MLA_HEADS * MLA_V
RET_HEADS = 4
RET_DK = 32
RET_DV = 64
RET_WIDTH = RET_HEADS * RET_DV
RET_CHUNK = 128
LRU_WIDTH = D_MODEL - MLA_WIDTH - RET_WIDTH
LRU_BLOCKS = 6
LRU_BLOCK = LRU_WIDTH // LRU_BLOCKS
CONV_WIDTH = 4
LRU_C = 8.0
FFN_HIDDEN = 2816
ROPE_BASE = 10000.0
NORM_EPS = 1e-6
LOG2_E = 1.4426950408889634
N_MOD = 6

LANES = 128
HEAD_PAD = LANES
MLA_PAD = MLA_HEADS * HEAD_PAD
Z_MLA = 512
Z_RET = 2 * RET_HEADS * RET_DK + 2 * RET_WIDTH
Z_LRU = 2 * LRU_WIDTH
FFN_CHUNK = 256
UNIT_ROWS = 256
UNITS_PER_STEP = 4
PREP_TOKENS = 1024
ATTN_BLOCK = 256
MOD_COLS = 1536
VMEM_LIMIT = 56 * 1024 * 1024

BF16 = jnp.bfloat16
F32 = jnp.float32


def _params(*sem):
    return pltpu.CompilerParams(dimension_semantics=sem, vmem_limit_bytes=VMEM_LIMIT)


def _resident(shape, index_map):
    return pl.BlockSpec(shape, index_map, pipeline_mode=pl.Buffered(1))


def _rms(x, g):
    return x * lax.rsqrt(jnp.mean(x * x, axis=-1, keepdims=True) + NORM_EPS) * g


def _swap_halves(x, first_half):
    return jnp.where(first_half, pltpu.roll(x, LANES - 16, 1), pltpu.roll(x, 16, 1))


def _mod_kernel(c_ref, w_ref, b_ref, o_ref):
    c = c_ref[...]
    ca = (c * jax.nn.sigmoid(c)).astype(BF16)
    o_ref[...] = jnp.dot(ca, w_ref[...].astype(BF16), preferred_element_type=F32) + b_ref[...]


def _modulation(c, w, b, tn):
    L, D, N = w.shape
    B = c.shape[0]
    return pl.pallas_call(
        _mod_kernel,
        grid=(L, N // tn),
        in_specs=[pl.BlockSpec((B, D), lambda l, j: (0, 0)),
                  pl.BlockSpec((None, D, tn), lambda l, j: (l, 0, j)),
                  pl.BlockSpec((None, 1, tn), lambda l, j: (l, 0, j))],
        out_specs=pl.BlockSpec((None, B, tn), lambda l, j: (l, 0, j)),
        out_shape=jax.ShapeDtypeStruct((L, B, N), F32),
        compiler_params=_params("parallel", "parallel"),
        name="modulation",
    )(c, w, b.reshape(L, 1, N))


def _rope_table_kernel(pos_ref, inv_ref, cos_ref, sin_ref, cost_ref, sint_ref):
    ang = inv_ref[...] * pos_ref[...].astype(F32)
    cos = jnp.cos(ang)
    sin = jnp.sin(ang)
    cost_ref[...] = cos
    sint_ref[...] = sin
    reps = LANES // (2 * cos.shape[0])
    cos_ref[...] = jnp.concatenate([cos] * (2 * reps), axis=0).T
    sin_ref[...] = jnp.concatenate([-sin, sin] * reps, axis=0).T


def _rope_tables(positions):
    B, S = positions.shape
    half = MLA_ROPE // 2
    inv = ROPE_BASE ** (-jnp.arange(half, dtype=F32) / half)
    ts = min(S, 1024)
    spec = pl.BlockSpec((None, ts, LANES), lambda b, i: (b, i, 0))
    spec_t = pl.BlockSpec((None, half, ts), lambda b, i: (b, 0, i))
    return pl.pallas_call(
        _rope_table_kernel,
        grid=(B, S // ts),
        in_specs=[pl.BlockSpec((None, 1, ts), lambda b, i: (b, 0, i)),
                  pl.BlockSpec((half, 1), lambda b, i: (0, 0))],
        out_specs=[spec, spec, spec_t, spec_t],
        out_shape=[jax.ShapeDtypeStruct((B, S, LANES), F32)] * 2
        + [jax.ShapeDtypeStruct((B, half, S), F32)] * 2,
        compiler_params=_params("parallel", "parallel"),
        name="rope_tables",
    )(positions.reshape(B, 1, S), inv.reshape(half, 1))


def _softplus(x):
    return jnp.maximum(x, 0.0) + jnp.log1p(jnp.exp(-jnp.abs(x)))


def _gelu_tanh(x):
    return 0.5 * x * (1.0 + jnp.tanh(0.7978845608028654 * (x + 0.044715 * (x * x * x))))


def _sigmoid(x):
    return 0.5 * jnp.tanh(0.5 * x) + 0.5


def _in_proj_lru_kernel(x_ref, mod_ref, g_ref, wm_ref, wr_ref, wl_ref, cw_ref, cb_ref, wa_ref,
                        ba_ref, wi_ref, bi_ref, lam_ref, zm_ref, zr_ref, yc_ref, xbuf, h_s,
                        *, parts):
    B, tt, D = x_ref.shape
    th = tt // parts
    W = LRU_WIDTH
    PAD = 8
    taps = CONV_WIDTH - 1

    @pl.when(pl.program_id(0) == 0)
    def _():
        xbuf[0:PAD] = jnp.zeros((PAD, B, W), F32)
        h_s[...] = jnp.zeros((B, W), F32)

    mod = mod_ref[...]
    cw = cw_ref[...]
    log_a_rate = -LRU_C * _softplus(-lam_ref[...])

    def part(k):
        tok = slice(k * th, (k + 1) * th)
        h = _rms(x_ref[:, tok, :], g_ref[...]) * (1.0 + mod[:, 1:2]) + mod[:, 0:1]
        h = h.reshape(B * th, D).astype(BF16)

        zl = jnp.dot(h, wl_ref[...], preferred_element_type=F32)
        zm_ref[:, tok, :] = jnp.dot(h, wm_ref[...],
                                    preferred_element_type=F32).reshape(B, th, Z_MLA)
        zl = pltpu.einshape("btd->tbd", zl.reshape(B, th, Z_LRU))
        gate = _gelu_tanh(zl[:, :, W:])

        xbuf[PAD:PAD + th] = zl[:, :, :W]
        xc = cb_ref[...].reshape(1, 1, W) + sum(
            xbuf[PAD - taps + j:PAD - taps + j + th] * cw[j].reshape(1, 1, W)
            for j in range(CONV_WIDTH))
        xbuf[PAD - taps:PAD] = xbuf[PAD + th - taps:PAD + th]

        xc2 = xc.reshape(th * B, W)
        xb = xc2.astype(BF16)
        r = _sigmoid(jnp.dot(xb, wa_ref[...], preferred_element_type=F32) + ba_ref[...])
        i = _sigmoid(jnp.dot(xb, wi_ref[...], preferred_element_type=F32) + bi_ref[...])
        zr_ref[:, tok, :] = jnp.dot(h, wr_ref[...],
                                    preferred_element_type=F32).reshape(B, th, Z_RET)

        log_a = log_a_rate * r
        a = jnp.exp(log_a)
        one_minus_a2 = -jnp.tanh(log_a) * (a * a + 1.0)
        root = jnp.where(one_minus_a2 > 0.0, one_minus_a2 * lax.rsqrt(one_minus_a2), 0.0)
        a = a.reshape(th, B, W)
        b = (root * i * xc2).reshape(th, B, W)

        hid = h_s[...]
        for t in range(th):
            hid = a[t] * hid + b[t]
            yc_ref[k * th + t] = gate[t] * hid
        h_s[...] = hid

    for k in range(parts):
        part(k)


def _in_proj_lru(x, mod, norm, w_groups, cw, cb, wa, ba, wi, bi, lam, l, tt, parts):
    B, S, D = x.shape
    W = LRU_WIDTH
    tok = lambda width: pl.BlockSpec((B, tt, width), lambda i: (0, i, 0))
    vec = pl.BlockSpec((None, 1, W), lambda i: (l, 0, 0))
    mat = _resident((None, W, W), lambda i: (l, 0, 0))
    return pl.pallas_call(
        functools.partial(_in_proj_lru_kernel, parts=parts),
        grid=(S // tt,),
        in_specs=[tok(D),
                  pl.BlockSpec((None, B, N_MOD, D), lambda i: (l, 0, 0, 0)),
                  pl.BlockSpec((None, 1, D), lambda i: (l, 0, 0)),
                  *[_resident((None, D, w.shape[-1]), lambda i: (l, 0, 0)) for w in w_groups],
                  pl.BlockSpec((None, CONV_WIDTH, W), lambda i: (l, 0, 0)),
                  vec, mat, vec, mat, vec, vec],
        out_specs=[tok(Z_MLA), tok(Z_RET),
                   pl.BlockSpec((tt, B, W), lambda i: (i, 0, 0))],
        out_shape=[jax.ShapeDtypeStruct((B, S, Z_MLA), F32),
                   jax.ShapeDtypeStruct((B, S, Z_RET), F32),
                   jax.ShapeDtypeStruct((S, B, W), F32)],
        scratch_shapes=[pltpu.VMEM((tt // parts + 8, B, W), F32), pltpu.VMEM((B, W), F32)],
        compiler_params=_params("arbitrary"),
        name="in_proj_lru",
    )(x, mod, norm, *w_groups, cw, cb, wa, ba, wi, bi, lam)


def _rms_feature_major(x, g):
    return x * lax.rsqrt(jnp.mean(x * x, axis=0, keepdims=True) + NORM_EPS) * g


def _mla_prep_kernel(z_ref, cos_ref, sin_ref, cost_ref, sint_ref, qn_ref, kvn_ref, kvnr_ref,
                     wqt_ref, wk_ref, wr_ref, wvt_ref, one_ref, qt_ref, k_ref, vt_ref, *, tq):
    z = z_ref[...]
    cos = cos_ref[...]
    sin = sin_ref[...]
    lane = lax.broadcasted_iota(jnp.int32, cos.shape, 1)
    first_half = (lane % 32) < 16

    c_kv = _rms(z[:, MLA_Q_RANK:MLA_Q_RANK + MLA_KV_RANK], kvnr_ref[...]).astype(BF16)
    kr = z[:, MLA_Q_RANK + MLA_KV_RANK:]
    kr = jnp.where(lane < MLA_ROPE, kr * cos + _swap_halves(kr, first_half) * sin, 0.0)
    k = (jnp.dot(c_kv, wk_ref[...], preferred_element_type=F32)
         + jnp.dot(kr.astype(BF16), wr_ref[...], preferred_element_type=F32))
    k_ref[...] = k.astype(BF16)

    zt = z[:, :MLA_Q_RANK + MLA_KV_RANK].T
    c_q = _rms_feature_major(zt[:MLA_Q_RANK], qn_ref[...]).astype(BF16)
    q = jnp.dot(wqt_ref[...], c_q, preferred_element_type=F32)
    qk_scale = (MLA_NOPE + MLA_ROPE) ** -0.5 * LOG2_E
    cos_t = cost_ref[...] * qk_scale
    sin_t = sint_ref[...] * qk_scale
    half = MLA_ROPE // 2
    rows = []
    for h in range(MLA_HEADS):
        r0 = h * HEAD_PAD + MLA_NOPE
        x1 = q[r0:r0 + half]
        x2 = q[r0 + half:r0 + MLA_ROPE]
        rows += [q[h * HEAD_PAD:r0] * qk_scale, x1 * cos_t - x2 * sin_t, x2 * cos_t + x1 * sin_t,
                 q[r0 + MLA_ROPE:(h + 1) * HEAD_PAD]]
    q = jnp.concatenate(rows, axis=0).astype(BF16)
    c_kv_t = _rms_feature_major(zt[MLA_Q_RANK:], kvn_ref[...]).astype(BF16)
    v = (jnp.dot(wvt_ref[...], c_kv_t, preferred_element_type=F32) + one_ref[...]).astype(BF16)
    for t in range(z.shape[0] // tq):
        qt_ref[t] = q[:, t * tq:(t + 1) * tq]
        vt_ref[t] = v[:, t * tq:(t + 1) * tq]


def _mla_prep(zm, tables, qn, kvn, wqt, wk, wr, wvt, ones, l, tm, tq):
    B, S, _ = zm.shape
    cos, sin, cos_t, sin_t = tables
    tok = lambda w: pl.BlockSpec((None, tm, w), lambda b, i: (b, i, 0))
    tab_t = pl.BlockSpec((None, MLA_ROPE // 2, tm), lambda b, i: (b, 0, i))
    col = lambda n: pl.BlockSpec((None, n, 1), lambda b, i: (l, 0, 0))
    tr_spec = pl.BlockSpec((None, tm // tq, MLA_PAD, tq), lambda b, i: (b, i, 0, 0))
    tr_shape = jax.ShapeDtypeStruct((B, S // tq, MLA_PAD, tq), BF16)
    L = qn.shape[0]
    return pl.pallas_call(
        functools.partial(_mla_prep_kernel, tq=tq),
        grid=(B, S // tm),
        in_specs=[tok(Z_MLA), tok(LANES), tok(LANES), tab_t, tab_t,
                  col(MLA_Q_RANK), col(MLA_KV_RANK),
                  pl.BlockSpec((None, 1, MLA_KV_RANK), lambda b, i: (l, 0, 0)),
                  _resident((None, MLA_PAD, MLA_Q_RANK), lambda b, i: (l, 0, 0)),
                  _resident((None, MLA_KV_RANK, MLA_PAD), lambda b, i: (l, 0, 0)),
                  _resident((LANES, MLA_PAD), lambda b, i: (0, 0)),
                  _resident((None, MLA_PAD, MLA_KV_RANK), lambda b, i: (l, 0, 0)),
                  pl.BlockSpec((MLA_PAD, 1), lambda b, i: (0, 0))],
        out_specs=[tr_spec, tok(MLA_PAD), tr_spec],
        out_shape=[tr_shape, jax.ShapeDtypeStruct((B, S, MLA_PAD), BF16), tr_shape],
        compiler_params=_params("parallel", "parallel"),
        name="mla_prep",
    )(zm, cos, sin, cos_t, sin_t, qn.reshape(L, -1, 1), kvn.reshape(L, -1, 1),
      kvn.reshape(L, 1, -1), wqt, wk, wr, wvt, ones)


def _retention_consts():
    C, H = RET_CHUNK, RET_HEADS
    log_g = jnp.log(1.0 - jnp.exp2(-5.0 - jnp.arange(H, dtype=F32)))
    idx = jnp.arange(C, dtype=F32)
    diff = idx[:, None] - idx[None, :]
    decay = jnp.where(diff >= 0, jnp.exp(log_g[:, None, None] * jnp.maximum(diff, 0.0)), 0.0)
    q_decay = jnp.exp(log_g[:, None] * (idx + 1.0))
    k_decay = jnp.exp(log_g[:, None] * (C - 1.0 - idx))
    chunk_decay = jnp.exp(log_g * C)
    qd = jnp.repeat(q_decay.T, RET_DK, axis=1)
    kd = jnp.repeat(k_decay.T, RET_DK, axis=1)
    cd = jnp.broadcast_to(jnp.repeat(chunk_decay, RET_DK)[:, None], (H * RET_DK, RET_WIDTH))
    return decay, qd, kd, cd


def _mixers_kernel(qt_ref, k_ref, vt_ref, z_ref, cos_ref, sin_ref, dec_ref, qd_ref, kd_ref,
                   cd_ref, gn_ref, ya_ref, yb_ref, s_s, state_s, *, tq, nq):
    groups = tq // 8
    key = lax.broadcasted_iota(jnp.int32, (tq, tq), 0)
    qry = lax.broadcasted_iota(jnp.int32, (tq, tq), 1)
    causal = key <= qry
    value_rows = lax.broadcasted_iota(jnp.int32, (HEAD_PAD, tq), 0) < MLA_V
    hs = lambda h: slice(h * HEAD_PAD, (h + 1) * HEAD_PAD)

    C = RET_CHUNK
    per_block = tq // C
    lane = lax.broadcasted_iota(jnp.int32, (C, LANES), 1)
    first_half = (lane % 32) < 16
    low = lane < RET_DV
    head_of_lane = lane // RET_DK
    srow = lax.broadcasted_iota(jnp.int32, (LANES, RET_WIDTH), 0) // RET_DK
    scol = lax.broadcasted_iota(jnp.int32, (LANES, RET_WIDTH), 1) // RET_DV
    same_head = srow == scol
    gn = gn_ref[...]
    cd = cd_ref[...]

    def rotate(x, rows):
        return x * cos_ref[rows, :] + _swap_halves(x, first_half) * sin_ref[rows, :]

    def block(i):
        rows = [slice((i * per_block + c) * C, (i * per_block + c + 1) * C)
                for c in range(per_block)]
        q = [rotate(z_ref[r, 0:LANES], r) for r in rows]
        k = [rotate(z_ref[r, LANES:2 * LANES], r) * (RET_DK ** -0.5) for r in rows]
        kb = [x.astype(BF16) for x in k]
        v = [z_ref[r, 2 * LANES:2 * LANES + RET_WIDTH].astype(BF16) for r in rows]

        def scores(h):
            m = None
            for j in range(i + 1):
                s = jnp.dot(k_ref[j * tq:(j + 1) * tq, hs(h)], qt_ref[i, hs(h), :],
                            preferred_element_type=F32)
                if j == i:
                    s = jnp.where(causal, s, -jnp.inf)
                s_s[h, j] = s
                mj = jnp.max(s.reshape(groups, 8, tq), axis=0)
                m = mj if m is None else jnp.maximum(m, mj)
            return jnp.broadcast_to(jnp.max(m, axis=0, keepdims=True), (8, tq))

        def values(h, m):
            acc = None
            for j in range(i + 1):
                p = jnp.exp2(s_s[h, j].reshape(groups, 8, tq) - m[None])
                pv = jnp.dot(vt_ref[j, hs(h), :], p.reshape(tq, tq).astype(BF16),
                             preferred_element_type=F32)
                acc = pv if acc is None else acc + pv
            return acc

        m_next = scores(0)
        sr = [[(lax.dot_general(jnp.where(head_of_lane == h, q[c], 0.0).astype(BF16), kb[c],
                                (((1,), (1,)), ((), ())), preferred_element_type=F32)
                * dec_ref[h]).astype(BF16) for h in range(RET_HEADS)] for c in range(per_block)]
        kv = [lax.dot_general((k[c] * kd_ref[...]).astype(BF16), v[c], (((0,), (0,)), ((), ())),
                              preferred_element_type=F32) for c in range(per_block)]
        acc = [None] * MLA_HEADS
        for h in range(MLA_HEADS):
            m = m_next
            if h + 1 < MLA_HEADS:
                m_next = scores(h + 1)
            acc[h] = values(h, m)
            if h == MLA_HEADS // 2 - 1:
                inner = [[jnp.where(low,
                                    jnp.dot(sr[c][2 * p], v[c][:, p * LANES:(p + 1) * LANES],
                                            preferred_element_type=F32),
                                    jnp.dot(sr[c][2 * p + 1], v[c][:, p * LANES:(p + 1) * LANES],
                                            preferred_element_type=F32))
                          for p in range(RET_HEADS // 2)] for c in range(per_block)]
                states = [jnp.zeros((LANES, RET_WIDTH), F32) if i == 0 else state_s[...]]
                for c in range(per_block):
                    states.append(states[c] * cd + jnp.where(same_head, kv[c], 0.0))
                state_s[...] = states[per_block]
                cross = [jnp.dot((q[c] * qd_ref[...]).astype(BF16), states[c].astype(BF16),
                                 preferred_element_type=F32) for c in range(per_block)]
            if h % 2 == 1:
                even, odd = acc[h - 1], acc[h]
                pair = jnp.where(value_rows, even / even[MLA_V:MLA_V + 1], odd / odd[0:1])
                ya_ref[i * tq:(i + 1) * tq, (h // 2) * LANES:(h // 2 + 1) * LANES] = (
                    pair.T.astype(BF16))

        for c in range(per_block):
            for p in range(RET_HEADS // 2):
                cols = slice(p * LANES, (p + 1) * LANES)
                o = inner[c][p] + cross[c][:, cols]
                s_lo = jnp.sum(jnp.where(low, o, 0.0), axis=-1, keepdims=True)
                s_hi = jnp.sum(jnp.where(low, 0.0, o), axis=-1, keepdims=True)
                d = o - jnp.where(low, s_lo, s_hi) * (1.0 / RET_DV)
                d2 = d * d
                v_lo = jnp.sum(jnp.where(low, d2, 0.0), axis=-1, keepdims=True)
                v_hi = jnp.sum(jnp.where(low, 0.0, d2), axis=-1, keepdims=True)
                var = jnp.where(low, v_lo, v_hi) * (1.0 / RET_DV)
                y = d * lax.rsqrt(var + NORM_EPS) * gn[:, cols]
                gp = z_ref[rows[c], 2 * LANES + RET_WIDTH + p * LANES:
                           2 * LANES + RET_WIDTH + (p + 1) * LANES]
                yb_ref[rows[c], cols] = (gp * jax.nn.sigmoid(gp) * y).astype(BF16)

    def body(i, carry):
        lax.switch(i, [functools.partial(block, n) for n in range(nq)])
        return carry

    lax.fori_loop(0, nq, body, 0)


def _mixers(qt, k, vt, zr, cos, sin, consts, gn, l, tq):
    B, S, _ = k.shape
    nq = S // tq
    decay, qd, kd, cd = consts
    tr_spec = pl.BlockSpec((None, nq, MLA_PAD, tq), lambda b: (b, 0, 0, 0))
    row = lambda width: pl.BlockSpec((None, S, width), lambda b: (b, 0, 0))
    full = lambda a: pl.BlockSpec(a.shape, lambda b: (0,) * a.ndim)
    return pl.pallas_call(
        functools.partial(_mixers_kernel, tq=tq, nq=nq),
        grid=(B,),
        in_specs=[tr_spec, row(MLA_PAD), tr_spec, row(Z_RET), row(LANES), row(LANES),
                  full(decay), full(qd), full(kd), full(cd),
                  pl.BlockSpec((None, 1, RET_WIDTH), lambda b: (l, 0, 0))],
        out_specs=[row(MLA_WIDTH), row(RET_WIDTH)],
        out_shape=[jax.ShapeDtypeStruct((B, S, MLA_WIDTH), BF16),
                   jax.ShapeDtypeStruct((B, S, RET_WIDTH), BF16)],
        scratch_shapes=[pltpu.VMEM((MLA_HEADS, nq, tq, tq), F32),
                        pltpu.VMEM((LANES, RET_WIDTH), F32)],
        compiler_params=_params("parallel"),
        name="mixers",
    )(qt, k, vt, zr, cos, sin, decay, qd, kd, cd, gn)


def _out_ffn_kernel(x_ref, ya_ref, yb_ref, yc_ref, mod_ref, g_ref, wo_ref, wg_ref, wu_ref,
                    wd_ref, fmod_ref, fg_ref, o_ref, h_s, acc_s, *, final, parts):
    B, tt, D = x_ref.shape
    th = tt // parts
    rows = B * th
    mod = mod_ref[...]
    a0, b0 = MLA_WIDTH, MLA_WIDTH + RET_WIDTH
    n_chunks = FFN_HIDDEN // FFN_CHUNK

    def out_proj(k):
        tok = slice(k * th, (k + 1) * th)
        r = slice(k * rows, (k + 1) * rows)
        yc = pltpu.einshape("tbd->btd", yc_ref[tok]).reshape(rows, LRU_WIDTH).astype(BF16)
        y = (jnp.dot(ya_ref[:, tok, :].reshape(rows, MLA_WIDTH), wo_ref[0:a0, :],
                     preferred_element_type=F32)
             + jnp.dot(yb_ref[:, tok, :].reshape(rows, RET_WIDTH), wo_ref[a0:b0, :],
                       preferred_element_type=F32)
             + jnp.dot(yc, wo_ref[b0:, :], preferred_element_type=F32))
        x1 = x_ref[:, tok, :] + mod[:, 2:3] * y.reshape(B, th, D)
        o_ref[:, tok, :] = x1
        h = _rms(x1, g_ref[...]) * (1.0 + mod[:, 4:5]) + mod[:, 3:4]
        h_s[r, :] = h.reshape(rows, D).astype(BF16)

    def ffn(k):
        tok = slice(k * th, (k + 1) * th)
        r = slice(k * rows, (k + 1) * rows)

        def gate_up(j):
            cols = slice(j * FFN_CHUNK, (j + 1) * FFN_CHUNK)
            h = h_s[r, :]
            return (jnp.dot(h, wg_ref[:, cols], preferred_element_type=F32),
                    jnp.dot(h, wu_ref[:, cols], preferred_element_type=F32))

        nxt = gate_up(0)
        for j in range(n_chunks):
            gate, up = nxt
            if j + 1 < n_chunks:
                nxt = gate_up(j + 1)
            act = (gate * jax.nn.sigmoid(gate) * up).astype(BF16)
            down = jnp.dot(act, wd_ref[j * FFN_CHUNK:(j + 1) * FFN_CHUNK, :],
                           preferred_element_type=F32)
            acc_s[r, :] = down if j == 0 else acc_s[r, :] + down
        x2 = o_ref[:, tok, :] + mod[:, 5:6] * acc_s[r, :].reshape(B, th, D)
        if final:
            fmod = fmod_ref[...]
            x2 = _rms(x2, fg_ref[...]) * (1.0 + fmod[:, 1:2]) + fmod[:, 0:1]
        o_ref[:, tok, :] = x2

    for k in range(parts):
        out_proj(k)
    for k in range(parts):
        ffn(k)


def _out_ffn(x, ya, yb, yc, mod, norm, wo, wgu, wd, fmod, fnorm, l, tt, parts, final):
    B, S, D = x.shape
    tok = lambda width: pl.BlockSpec((B, tt, width), lambda i: (0, i, 0))
    return pl.pallas_call(
        functools.partial(_out_ffn_kernel, final=final, parts=parts),
        grid=(S // tt,),
        in_specs=[tok(D), tok(MLA_WIDTH), tok(RET_WIDTH),
                  pl.BlockSpec((tt, B, LRU_WIDTH), lambda i: (i, 0, 0)),
                  pl.BlockSpec((None, B, N_MOD, D), lambda i: (l, 0, 0, 0)),
                  pl.BlockSpec((None, 1, D), lambda i: (l, 0, 0)),
                  _resident((None, D, D), lambda i: (l, 0, 0)),
                  _resident((None, D, FFN_HIDDEN), lambda i: (l, 0, 0)),
                  _resident((None, D, FFN_HIDDEN), lambda i: (l, 0, 1)),
                  _resident((None, FFN_HIDDEN, D), lambda i: (l, 0, 0)),
                  pl.BlockSpec((B, 2, D), lambda i: (0, 0, 0)),
                  pl.BlockSpec((1, D), lambda i: (0, 0))],
        out_specs=tok(D),
        out_shape=jax.ShapeDtypeStruct((B, S, D), F32),
        scratch_shapes=[pltpu.VMEM((B * tt, D), BF16), pltpu.VMEM((B * tt, D), F32)],
        compiler_params=_params("parallel"),
        name="out_ffn",
    )(x, ya, yb, yc, mod, norm, wo, wgu, wgu, wd, fmod, fnorm)


def _split_in_proj(w_in):
    n_mla = MLA_Q_RANK + MLA_KV_RANK + MLA_ROPE
    w_mla = jnp.pad(w_in[..., :n_mla].astype(BF16), ((0, 0), (0, 0), (0, Z_MLA - n_mla)))
    return (w_mla, w_in[..., n_mla:n_mla + Z_RET].astype(BF16),
            w_in[..., n_mla + Z_RET:].astype(BF16))


def _pad_heads(w, width):
    L, R, H, _ = w.shape
    return jnp.pad(w, ((0, 0), (0, 0), (0, 0), (0, HEAD_PAD - width))).reshape(L, R, H * HEAD_PAD)


def _mla_weights(w_uq, w_ukv):
    L = w_uq.shape[0]
    wq = _pad_heads(w_uq.reshape(L, MLA_Q_RANK, MLA_HEADS, MLA_NOPE + MLA_ROPE), MLA_NOPE + MLA_ROPE)
    kv = w_ukv.reshape(L, MLA_KV_RANK, MLA_HEADS, MLA_NOPE + MLA_V)
    wk = _pad_heads(kv[..., :MLA_NOPE], MLA_NOPE)
    v = kv[..., MLA_NOPE:]
    odd = (jnp.arange(MLA_HEADS) % 2 == 1)[None, None, :, None]
    zero = jnp.zeros_like(v)
    wv = jnp.concatenate([jnp.where(odd, zero, v), jnp.where(odd, v, zero)], axis=-1)
    wv = wv.reshape(L, MLA_KV_RANK, MLA_PAD)
    r = jnp.arange(LANES)[:, None]
    c = jnp.arange(MLA_PAD)[None, :]
    wr = ((r < MLA_ROPE) & (c % HEAD_PAD == MLA_NOPE + r)).astype(BF16)
    ones = ((c % HEAD_PAD) == jnp.where((c // HEAD_PAD) % 2 == 0, MLA_V, 0)).astype(F32)
    wqt = jnp.swapaxes(wq, 1, 2).astype(BF16)
    wvt = jnp.swapaxes(wv, 1, 2).astype(BF16)
    return wqt, wk.astype(BF16), wr, wvt, ones.reshape(MLA_PAD, 1)


def _block_diag(w):
    L, G, I, J = w.shape
    eye = jnp.eye(G, dtype=w.dtype)
    return jnp.einsum('lgij,gh->lgihj', w, eye).reshape(L, G * I, G * J)


def kernel(x, c, positions, mod_w, mod_b, norm1, w_in, mla_q_norm, mla_w_uq, mla_kv_norm, mla_w_ukv, ret_gn, lru_conv_w, lru_conv_b, lru_w_a, lru_b_a, lru_w_i, lru_b_i, lru_lambda, w_out, norm2, w_gate_up, w_down, final_norm, final_mod_w, final_mod_b):
    B, S, D = x.shape
    L = mod_w.shape[0]
    tm = min(S, UNIT_ROWS)
    tt = UNITS_PER_STEP * tm // B
    tp = min(S, PREP_TOKENS)
    tq = min(S, ATTN_BLOCK)
    row = lambda a: a.reshape(L, 1, a.shape[-1])

    mod = _modulation(c, mod_w, mod_b, MOD_COLS).reshape(L, B, N_MOD, D)
    fmod = _modulation(c, final_mod_w[None], final_mod_b[None], D).reshape(B, 2, D)
    fnorm = final_norm.reshape(1, D)
    tables = _rope_tables(positions)

    w_in_groups = _split_in_proj(w_in)
    wqt, wk, wr, wvt, ones = _mla_weights(mla_w_uq, mla_w_ukv)
    wa = _block_diag(lru_w_a).astype(BF16)
    wi = _block_diag(lru_w_i).astype(BF16)
    wo = w_out.astype(BF16)
    wgu = w_gate_up.astype(BF16)
    wd = w_down.astype(BF16)
    ret_consts = _retention_consts()

    for l in range(L):
        zm, zr, yc = _in_proj_lru(x, mod, row(norm1), w_in_groups, lru_conv_w, row(lru_conv_b), wa,
                                  row(lru_b_a), wi, row(lru_b_i), row(lru_lambda), l, tt,
                                  UNITS_PER_STEP)
        qt, k, vt = _mla_prep(zm, tables, mla_q_norm, mla_kv_norm, wqt, wk, wr, wvt, ones,
                              l, tp, tq)
        ya, yb = _mixers(qt, k, vt, zr, tables[0], tables[1], ret_consts, row(ret_gn), l, tq)
        x = _out_ffn(x, ya, yb, yc, mod, row(norm2), wo, wgu, wd, fmod, fnorm, l, tt,
                     UNITS_PER_STEP, final=(l == L - 1))
    return x
```

```python
import functools

import jax
import jax.numpy as jnp
from jax import lax
from jax.experimental import pallas as pl
from jax.experimental.pallas import tpu as pltpu

D_MODEL = 1024
MLA_HEADS = 6
MLA_Q_RANK = 256
MLA_KV_RANK = 128
MLA_NOPE = 64
MLA_ROPE = 32
MLA_V = 64
MLA_WIDTH = MLA_HEADS * MLA_V
RET_HEADS = 4
RET_DK = 32
RET_DV = 64
RET_WIDTH = RET_HEADS * RET_DV
RET_CHUNK = 128
LRU_WIDTH = D_MODEL - MLA_WIDTH - RET_WIDTH
LRU_BLOCKS = 6
LRU_BLOCK = LRU_WIDTH // LRU_BLOCKS
CONV_WIDTH = 4
LRU_C = 8.0
FFN_HIDDEN = 2816
ROPE_BASE = 10000.0
NORM_EPS = 1e-6
LOG2_E = 1.4426950408889634
N_MOD = 6

LANES = 128
HEAD_PAD = LANES
MLA_PAD = MLA_HEADS * HEAD_PAD
Z_MLA = 512
Z_RET = 2 * RET_HEADS * RET_DK + 2 * RET_WIDTH
Z_LRU = 2 * LRU_WIDTH
FFN_CHUNK = 256
UNIT_ROWS = 256
UNITS_PER_STEP = 4
PREP_TOKENS = 1024
ATTN_BLOCK = 256
MOD_COLS = 1536
VMEM_LIMIT = 56 * 1024 * 1024

BF16 = jnp.bfloat16
F32 = jnp.float32


def _params(*sem):
    return pltpu.CompilerParams(dimension_semantics=sem, vmem_limit_bytes=VMEM_LIMIT)


def _resident(shape, index_map):
    return pl.BlockSpec(shape, index_map, pipeline_mode=pl.Buffered(1))


def _rms(x, g):
    return x * lax.rsqrt(jnp.mean(x * x, axis=-1, keepdims=True) + NORM_EPS) * g


def _swap_halves(x, first_half):
    return jnp.where(first_half, pltpu.roll(x, LANES - 16, 1), pltpu.roll(x, 16, 1))


def _mod_kernel(c_ref, w_ref, b_ref, o_ref):
    c = c_ref[...]
    ca = (c * jax.nn.sigmoid(c)).astype(BF16)
    o_ref[...] = jnp.dot(ca, w_ref[...].astype(BF16), preferred_element_type=F32) + b_ref[...]


def _modulation(c, w, b, tn):
    L, D, N = w.shape
    B = c.shape[0]
    return pl.pallas_call(
        _mod_kernel,
        grid=(L, N // tn),
        in_specs=[pl.BlockSpec((B, D), lambda l, j: (0, 0)),
                  pl.BlockSpec((None, D, tn), lambda l, j: (l, 0, j)),
                  pl.BlockSpec((None, 1, tn), lambda l, j: (l, 0, j))],
        out_specs=pl.BlockSpec((None, B, tn), lambda l, j: (l, 0, j)),
        out_shape=jax.ShapeDtypeStruct((L, B, N), F32),
        compiler_params=_params("parallel", "parallel"),
        name="modulation",
    )(c, w, b.reshape(L, 1, N))


def _rope_table_kernel(pos_ref, inv_ref, cos_ref, sin_ref, cost_ref, sint_ref):
    ang = inv_ref[...] * pos_ref[...].astype(F32)
    cos = jnp.cos(ang)
    sin = jnp.sin(ang)
    cost_ref[...] = cos
    sint_ref[...] = sin
    reps = LANES // (2 * cos.shape[0])
    cos_ref[...] = jnp.concatenate([cos] * (2 * reps), axis=0).T
    sin_ref[...] = jnp.concatenate([-sin, sin] * reps, axis=0).T


def _rope_tables(positions):
    B, S = positions.shape
    half = MLA_ROPE // 2
    inv = ROPE_BASE ** (-jnp.arange(half, dtype=F32) / half)
    ts = min(S, 1024)
    spec = pl.BlockSpec((None, ts, LANES), lambda b, i: (b, i, 0))
    spec_t = pl.BlockSpec((None, half, ts), lambda b, i: (b, 0, i))
    return pl.pallas_call(
        _rope_table_kernel,
        grid=(B, S // ts),
        in_specs=[pl.BlockSpec((None, 1, ts), lambda b, i: (b, 0, i)),
                  pl.BlockSpec((half, 1), lambda b, i: (0, 0))],
        out_specs=[spec, spec, spec_t, spec_t],
        out_shape=[jax.ShapeDtypeStruct((B, S, LANES), F32)] * 2
        + [jax.ShapeDtypeStruct((B, half, S), F32)] * 2,
        compiler_params=_params("parallel", "parallel"),
        name="rope_tables",
    )(positions.reshape(B, 1, S), inv.reshape(half, 1))


def _softplus(x):
    return jnp.maximum(x, 0.0) + jnp.log1p(jnp.exp(-jnp.abs(x)))


def _gelu_tanh(x):
    return 0.5 * x * (1.0 + jnp.tanh(0.7978845608028654 * (x + 0.044715 * (x * x * x))))


def _sigmoid(x):
    return 0.5 * jnp.tanh(0.5 * x) + 0.5


def _in_proj_lru_kernel(x_ref, mod_ref, g_ref, wm_ref, wr_ref, wl_ref, cw_ref, cb_ref, wa_ref,
                        ba_ref, wi_ref, bi_ref, lam_ref, zm_ref, zr_ref, yc_ref, xbuf, h_s,
                        *, parts):
    B, tt, D = x_ref.shape
    th = tt // parts
    W = LRU_WIDTH
    PAD = 8
    taps = CONV_WIDTH - 1

    @pl.when(pl.program_id(0) == 0)
    def _():
        xbuf[0:PAD] = jnp.zeros((PAD, B, W), F32)
        h_s[...] = jnp.zeros((B, W), F32)

    mod = mod_ref[...]
    cw = cw_ref[...]
    log_a_rate = -LRU_C * _softplus(-lam_ref[...])

    def part(k):
        tok = slice(k * th, (k + 1) * th)
        h = _rms(x_ref[:, tok, :], g_ref[...]) * (1.0 + mod[:, 1:2]) + mod[:, 0:1]
        h = h.reshape(B * th, D).astype(BF16)

        zl = jnp.dot(h, wl_ref[...], preferred_element_type=F32)
        zm_ref[:, tok, :] = jnp.dot(h, wm_ref[...],
                                    preferred_element_type=F32).reshape(B, th, Z_MLA)
        zl = pltpu.einshape("btd->tbd", zl.reshape(B, th, Z_LRU))
        gate = _gelu_tanh(zl[:, :, W:])

        xbuf[PAD:PAD + th] = zl[:, :, :W]
        xc = cb_ref[...].reshape(1, 1, W) + sum(
            xbuf[PAD - taps + j:PAD - taps + j + th] * cw[j].reshape(1, 1, W)
            for j in range(CONV_WIDTH))
        xbuf[PAD - taps:PAD] = xbuf[PAD + th - taps:PAD + th]

        xc2 = xc.reshape(th * B, W)
        xb = xc2.astype(BF16)
        r = _sigmoid(jnp.dot(xb, wa_ref[...], preferred_element_type=F32) + ba_ref[...])
        i = _sigmoid(jnp.dot(xb, wi_ref[...], preferred_element_type=F32) + bi_ref[...])
        zr_ref[:, tok, :] = jnp.dot(h, wr_ref[...],
                                    preferred_element_type=F32).reshape(B, th, Z_RET)

        log_a = log_a_rate * r
        a = jnp.exp(log_a)
        one_minus_a2 = -jnp.tanh(log_a) * (a * a + 1.0)
        root = jnp.where(one_minus_a2 > 0.0, one_minus_a2 * lax.rsqrt(one_minus_a2), 0.0)
        a = a.reshape(th, B, W)
        b = (root * i * xc2).reshape(th, B, W)

        hid = h_s[...]
        for t in range(th):
            hid = a[t] * hid + b[t]
            yc_ref[k * th + t] = gate[t] * hid
        h_s[...] = hid

    for k in range(parts):
        part(k)


def _in_proj_lru(x, mod, norm, w_groups, cw, cb, wa, ba, wi, bi, lam, l, tt, parts):
    B, S, D = x.shape
    W = LRU_WIDTH
    tok = lambda width: pl.BlockSpec((B, tt, width), lambda i: (0, i, 0))
    vec = pl.BlockSpec((None, 1, W), lambda i: (l, 0, 0))
    mat = _resident((None, W, W), lambda i: (l, 0, 0))
    return pl.pallas_call(
        functools.partial(_in_proj_lru_kernel, parts=parts),
        grid=(S // tt,),
        in_specs=[tok(D),
                  pl.BlockSpec((None, B, N_MOD, D), lambda i: (l, 0, 0, 0)),
                  pl.BlockSpec((None, 1, D), lambda i: (l, 0, 0)),
                  *[_resident((None, D, w.shape[-1]), lambda i: (l, 0, 0)) for w in w_groups],
                  pl.BlockSpec((None, CONV_WIDTH, W), lambda i: (l, 0, 0)),
                  vec, mat, vec, mat, vec, vec],
        out_specs=[tok(Z_MLA), tok(Z_RET),
                   pl.BlockSpec((tt, B, W), lambda i: (i, 0, 0))],
        out_shape=[jax.ShapeDtypeStruct((B, S, Z_MLA), F32),
                   jax.ShapeDtypeStruct((B, S, Z_RET), F32),
                   jax.ShapeDtypeStruct((S, B, W), F32)],
        scratch_shapes=[pltpu.VMEM((tt // parts + 8, B, W), F32), pltpu.VMEM((B, W), F32)],
        compiler_params=_params("arbitrary"),
        name="in_proj_lru",
    )(x, mod, norm, *w_groups, cw, cb, wa, ba, wi, bi, lam)


def _rms_feature_major(x, g):
    return x * lax.rsqrt(jnp.mean(x * x, axis=0, keepdims=True) + NORM_EPS) * g


def _mla_prep_kernel(z_ref, cos_ref, sin_ref, cost_ref, sint_ref, qn_ref, kvn_ref, kvnr_ref,
                     wqt_ref, wk_ref, wr_ref, wvt_ref, one_ref, qt_ref, k_ref, vt_ref, *, tq):
    z = z_ref[...]
    cos = cos_ref[...]
    sin = sin_ref[...]
    lane = lax.broadcasted_iota(jnp.int32, cos.shape, 1)
    first_half = (lane % 32) < 16

    c_kv = _rms(z[:, MLA_Q_RANK:MLA_Q_RANK + MLA_KV_RANK], kvnr_ref[...]).astype(BF16)
    kr = z[:, MLA_Q_RANK + MLA_KV_RANK:]
    kr = jnp.where(lane < MLA_ROPE, kr * cos + _swap_halves(kr, first_half) * sin, 0.0)
    k = (jnp.dot(c_kv, wk_ref[...], preferred_element_type=F32)
         + jnp.dot(kr.astype(BF16), wr_ref[...], preferred_element_type=F32))
    k_ref[...] = k.astype(BF16)

    zt = z[:, :MLA_Q_RANK + MLA_KV_RANK].T
    c_q = _rms_feature_major(zt[:MLA_Q_RANK], qn_ref[...]).astype(BF16)
    q = jnp.dot(wqt_ref[...], c_q, preferred_element_type=F32)
    qk_scale = (MLA_NOPE + MLA_ROPE) ** -0.5 * LOG2_E
    cos_t = cost_ref[...] * qk_scale
    sin_t = sint_ref[...] * qk_scale
    half = MLA_ROPE // 2
    rows = []
    for h in range(MLA_HEADS):
        r0 = h * HEAD_PAD + MLA_NOPE
        x1 = q[r0:r0 + half]
        x2 = q[r0 + half:r0 + MLA_ROPE]
        rows += [q[h * HEAD_PAD:r0] * qk_scale, x1 * cos_t - x2 * sin_t, x2 * cos_t + x1 * sin_t,
                 q[r0 + MLA_ROPE:(h + 1) * HEAD_PAD]]
    q = jnp.concatenate(rows, axis=0).astype(BF16)
    c_kv_t = _rms_feature_major(zt[MLA_Q_RANK:], kvn_ref[...]).astype(BF16)
    v = (jnp.dot(wvt_ref[...], c_kv_t, preferred_element_type=F32) + one_ref[...]).astype(BF16)
    for t in range(z.shape[0] // tq):
        qt_ref[t] = q[:, t * tq:(t + 1) * tq]
        vt_ref[t] = v[:, t * tq:(t + 1) * tq]


def _mla_prep(zm, tables, qn, kvn, wqt, wk, wr, wvt, ones, l, tm, tq):
    B, S, _ = zm.shape
    cos, sin, cos_t, sin_t = tables
    tok = lambda w: pl.BlockSpec((None, tm, w), lambda b, i: (b, i, 0))
    tab_t = pl.BlockSpec((None, MLA_ROPE // 2, tm), lambda b, i: (b, 0, i))
    col = lambda n: pl.BlockSpec((None, n, 1), lambda b, i: (l, 0, 0))
    tr_spec = pl.BlockSpec((None, tm // tq, MLA_PAD, tq), lambda b, i: (b, i, 0, 0))
    tr_shape = jax.ShapeDtypeStruct((B, S // tq, MLA_PAD, tq), BF16)
    L = qn.shape[0]
    return pl.pallas_call(
        functools.partial(_mla_prep_kernel, tq=tq),
        grid=(B, S // tm),
        in_specs=[tok(Z_MLA), tok(LANES), tok(LANES), tab_t, tab_t,
                  col(MLA_Q_RANK), col(MLA_KV_RANK),
                  pl.BlockSpec((None, 1, MLA_KV_RANK), lambda b, i: (l, 0, 0)),
                  _resident((None, MLA_PAD, MLA_Q_RANK), lambda b, i: (l, 0, 0)),
                  _resident((None, MLA_KV_RANK, MLA_PAD), lambda b, i: (l, 0, 0)),
                  _resident((LANES, MLA_PAD), lambda b, i: (0, 0)),
                  _resident((None, MLA_PAD, MLA_KV_RANK), lambda b, i: (l, 0, 0)),
                  pl.BlockSpec((MLA_PAD, 1), lambda b, i: (0, 0))],
        out_specs=[tr_spec, tok(MLA_PAD), tr_spec],
        out_shape=[tr_shape, jax.ShapeDtypeStruct((B, S, MLA_PAD), BF16), tr_shape],
        compiler_params=_params("parallel", "parallel"),
        name="mla_prep",
    )(zm, cos, sin, cos_t, sin_t, qn.reshape(L, -1, 1), kvn.reshape(L, -1, 1),
      kvn.reshape(L, 1, -1), wqt, wk, wr, wvt, ones)


def _retention_consts():
    C, H = RET_CHUNK, RET_HEADS
    log_g = jnp.log(1.0 - jnp.exp2(-5.0 - jnp.arange(H, dtype=F32)))
    idx = jnp.arange(C, dtype=F32)
    diff = idx[:, None] - idx[None, :]
    decay = jnp.where(diff >= 0, jnp.exp(log_g[:, None, None] * jnp.maximum(diff, 0.0)), 0.0)
    q_decay = jnp.exp(log_g[:, None] * (idx + 1.0))
    k_decay = jnp.exp(log_g[:, None] * (C - 1.0 - idx))
    chunk_decay = jnp.exp(log_g * C)
    qd = jnp.repeat(q_decay.T, RET_DK, axis=1)
    kd = jnp.repeat(k_decay.T, RET_DK, axis=1)
    cd = jnp.broadcast_to(jnp.repeat(chunk_decay, RET_DK)[:, None], (H * RET_DK, RET_WIDTH))
    return decay, qd, kd, cd


def _mixers_kernel(qt_ref, k_ref, vt_ref, z_ref, cos_ref, sin_ref, dec_ref, qd_ref, kd_ref,
                   cd_ref, gn_ref, ya_ref, yb_ref, s_s, state_s, acc_s, o_s, rq_s, rk_s,
                   *, tq, nq):
    groups = tq // 8
    key = lax.broadcasted_iota(jnp.int32, (tq, tq), 0)
    qry = lax.broadcasted_iota(jnp.int32, (tq, tq), 1)
    causal = key <= qry
    value_rows = lax.broadcasted_iota(jnp.int32, (HEAD_PAD, tq), 0) < MLA_V
    hs = lambda h: slice(h * HEAD_PAD, (h + 1) * HEAD_PAD)

    C = RET_CHUNK
    per_block = tq // C
    lane = lax.broadcasted_iota(jnp.int32, (C, LANES), 1)
    first_half = (lane % 32) < 16
    low = lane < RET_DV
    head_of_lane = lane // RET_DK
    srow = lax.broadcasted_iota(jnp.int32, (LANES, RET_WIDTH), 0) // RET_DK
    scol = lax.broadcasted_iota(jnp.int32, (LANES, RET_WIDTH), 1) // RET_DV
    same_head = srow == scol
    gn = gn_ref[...]
    cd = cd_ref[...]

    def chunk_rows(ib):
        return [slice((ib * per_block + c) * C, (ib * per_block + c + 1) * C)
                for c in range(per_block)]

    def retention_prepare(ib):
        for c, r in enumerate(chunk_rows(ib)):
            cos = cos_ref[r, :]
            sin = sin_ref[r, :]
            xq = z_ref[r, 0:LANES]
            xk = z_ref[r, LANES:2 * LANES]
            rq_s[c] = xq * cos + _swap_halves(xq, first_half) * sin
            rk_s[c] = (xk * cos + _swap_halves(xk, first_half) * sin) * (RET_DK ** -0.5)

    def attention_finish(ib):
        for p in range(MLA_HEADS // 2):
            even = acc_s[2 * p]
            odd = acc_s[2 * p + 1]
            pair = jnp.where(value_rows, even / even[MLA_V:MLA_V + 1], odd / odd[0:1])
            ya_ref[ib * tq:(ib + 1) * tq, p * LANES:(p + 1) * LANES] = pair.T.astype(BF16)

    def retention_finish(ib):
        for c, r in enumerate(chunk_rows(ib)):
            for p in range(RET_HEADS // 2):
                cols = slice(p * LANES, (p + 1) * LANES)
                o = o_s[c, :, cols]
                s_lo = jnp.sum(jnp.where(low, o, 0.0), axis=-1, keepdims=True)
                s_hi = jnp.sum(jnp.where(low, 0.0, o), axis=-1, keepdims=True)
                d = o - jnp.where(low, s_lo, s_hi) * (1.0 / RET_DV)
                d2 = d * d
                v_lo = jnp.sum(jnp.where(low, d2, 0.0), axis=-1, keepdims=True)
                v_hi = jnp.sum(jnp.where(low, 0.0, d2), axis=-1, keepdims=True)
                var = jnp.where(low, v_lo, v_hi) * (1.0 / RET_DV)
                y = d * lax.rsqrt(var + NORM_EPS) * gn[:, cols]
                gp = z_ref[r, 2 * LANES + RET_WIDTH + p * LANES:
                           2 * LANES + RET_WIDTH + (p + 1) * LANES]
                yb_ref[r, cols] = (gp * jax.nn.sigmoid(gp) * y).astype(BF16)

    def block(i):
        if i == 0:
            retention_prepare(0)
        rows = chunk_rows(i)
        q = [rq_s[c] for c in range(per_block)]
        k = [rk_s[c] for c in range(per_block)]
        kb = [x.astype(BF16) for x in k]
        v = [z_ref[r, 2 * LANES:2 * LANES + RET_WIDTH].astype(BF16) for r in rows]
        if i > 0:
            attention_finish(i - 1)
            retention_finish(i - 1)

        def scores(h):
            m = None
            for j in range(i + 1):
                s = jnp.dot(k_ref[j * tq:(j + 1) * tq, hs(h)], qt_ref[i, hs(h), :],
                            preferred_element_type=F32)
                if j == i:
                    s = jnp.where(causal, s, -jnp.inf)
                s_s[h, j] = s
                mj = jnp.max(s.reshape(groups, 8, tq), axis=0)
                m = mj if m is None else jnp.maximum(m, mj)
            return jnp.broadcast_to(jnp.max(m, axis=0, keepdims=True), (8, tq))

        def values(h, m):
            acc = None
            for j in range(i + 1):
                p = jnp.exp2(s_s[h, j].reshape(groups, 8, tq) - m[None])
                pv = jnp.dot(vt_ref[j, hs(h), :], p.reshape(tq, tq).astype(BF16),
                             preferred_element_type=F32)
                acc = pv if acc is None else acc + pv
            return acc

        m_next = scores(0)
        sr = [[(lax.dot_general(jnp.where(head_of_lane == h, q[c], 0.0).astype(BF16), kb[c],
                                (((1,), (1,)), ((), ())), preferred_element_type=F32)
                * dec_ref[h]).astype(BF16) for h in range(RET_HEADS)] for c in range(per_block)]
        kv = [lax.dot_general((k[c] * kd_ref[...]).astype(BF16), v[c], (((0,), (0,)), ((), ())),
                              preferred_element_type=F32) for c in range(per_block)]
        for h in range(MLA_HEADS):
            m = m_next
            if h + 1 < MLA_HEADS:
                m_next = scores(h + 1)
            acc_s[h] = values(h, m)
            if h == MLA_HEADS // 2 - 1:
                states = [jnp.zeros((LANES, RET_WIDTH), F32) if i == 0 else state_s[...]]
                for c in range(per_block):
                    states.append(states[c] * cd + jnp.where(same_head, kv[c], 0.0))
                state_s[...] = states[per_block]
                for c in range(per_block):
                    cross = jnp.dot((q[c] * qd_ref[...]).astype(BF16), states[c].astype(BF16),
                                    preferred_element_type=F32)
                    for p in range(RET_HEADS // 2):
                        cols = slice(p * LANES, (p + 1) * LANES)
                        inner = jnp.where(
                            low,
                            jnp.dot(sr[c][2 * p], v[c][:, cols], preferred_element_type=F32),
                            jnp.dot(sr[c][2 * p + 1], v[c][:, cols],
                                    preferred_element_type=F32))
                        o_s[c, :, cols] = inner + cross[:, cols]
        if i + 1 < nq:
            retention_prepare(i + 1)

    def body(i, carry):
        lax.switch(i, [functools.partial(block, n) for n in range(nq)])
        return carry

    lax.fori_loop(0, nq, body, 0)
    attention_finish(nq - 1)
    retention_finish(nq - 1)


def _mixers(qt, k, vt, zr, cos, sin, consts, gn, l, tq):
    B, S, _ = k.shape
    nq = S // tq
    decay, qd, kd, cd = consts
    tr_spec = pl.BlockSpec((None, nq, MLA_PAD, tq), lambda b: (b, 0, 0, 0))
    row = lambda width: pl.BlockSpec((None, S, width), lambda b: (b, 0, 0))
    full = lambda a: pl.BlockSpec(a.shape, lambda b: (0,) * a.ndim)
    return pl.pallas_call(
        functools.partial(_mixers_kernel, tq=tq, nq=nq),
        grid=(B,),
        in_specs=[tr_spec, row(MLA_PAD), tr_spec, row(Z_RET), row(LANES), row(LANES),
                  full(decay), full(qd), full(kd), full(cd),
                  pl.BlockSpec((None, 1, RET_WIDTH), lambda b: (l, 0, 0))],
        out_specs=[row(MLA_WIDTH), row(RET_WIDTH)],
        out_shape=[jax.ShapeDtypeStruct((B, S, MLA_WIDTH), BF16),
                   jax.ShapeDtypeStruct((B, S, RET_WIDTH), BF16)],
        scratch_shapes=[pltpu.VMEM((MLA_HEADS, nq, tq, tq), F32),
                        pltpu.VMEM((LANES, RET_WIDTH), F32),
                        pltpu.VMEM((MLA_HEADS, HEAD_PAD, tq), F32),
                        pltpu.VMEM((tq // RET_CHUNK, RET_CHUNK, RET_WIDTH), F32),
                        pltpu.VMEM((tq // RET_CHUNK, RET_CHUNK, LANES), F32),
                        pltpu.VMEM((tq // RET_CHUNK, RET_CHUNK, LANES), F32)],
        compiler_params=_params("parallel"),
        name="mixers",
    )(qt, k, vt, zr, cos, sin, decay, qd, kd, cd, gn)


def _out_ffn_kernel(x_ref, ya_ref, yb_ref, yc_ref, mod_ref, g_ref, wo_ref, wg_ref, wu_ref,
                    wd_ref, fmod_ref, fg_ref, o_ref, h_s, acc_s, *, final, parts):
    B, tt, D = x_ref.shape
    th = tt // parts
    rows = B * th
    mod = mod_ref[...]
    a0, b0 = MLA_WIDTH, MLA_WIDTH + RET_WIDTH
    n_chunks = FFN_HIDDEN // FFN_CHUNK

    def out_proj(k):
        tok = slice(k * th, (k + 1) * th)
        r = slice(k * rows, (k + 1) * rows)
        yc = pltpu.einshape("tbd->btd", yc_ref[tok]).reshape(rows, LRU_WIDTH).astype(BF16)
        y = (jnp.dot(ya_ref[:, tok, :].reshape(rows, MLA_WIDTH), wo_ref[0:a0, :],
                     preferred_element_type=F32)
             + jnp.dot(yb_ref[:, tok, :].reshape(rows, RET_WIDTH), wo_ref[a0:b0, :],
                       preferred_element_type=F32)
             + jnp.dot(yc, wo_ref[b0:, :], preferred_element_type=F32))
        x1 = x_ref[:, tok, :] + mod[:, 2:3] * y.reshape(B, th, D)
        o_ref[:, tok, :] = x1
        h = _rms(x1, g_ref[...]) * (1.0 + mod[:, 4:5]) + mod[:, 3:4]
        h_s[r, :] = h.reshape(rows, D).astype(BF16)

    def ffn(k):
        tok = slice(k * th, (k + 1) * th)
        r = slice(k * rows, (k + 1) * rows)

        def gate_up(j):
            cols = slice(j * FFN_CHUNK, (j + 1) * FFN_CHUNK)
            h = h_s[r, :]
            return (jnp.dot(h, wg_ref[:, cols], preferred_element_type=F32),
                    jnp.dot(h, wu_ref[:, cols], preferred_element_type=F32))

        nxt = gate_up(0)
        for j in range(n_chunks):
            gate, up = nxt
            if j + 1 < n_chunks:
                nxt = gate_up(j + 1)
            act = (gate * jax.nn.sigmoid(gate) * up).astype(BF16)
            down = jnp.dot(act, wd_ref[j * FFN_CHUNK:(j + 1) * FFN_CHUNK, :],
                           preferred_element_type=F32)
            acc_s[r, :] = down if j == 0 else acc_s[r, :] + down
        x2 = o_ref[:, tok, :] + mod[:, 5:6] * acc_s[r, :].reshape(B, th, D)
        if final:
            fmod = fmod_ref[...]
            x2 = _rms(x2, fg_ref[...]) * (1.0 + fmod[:, 1:2]) + fmod[:, 0:1]
        o_ref[:, tok, :] = x2

    for k in range(parts):
        out_proj(k)
    for k in range(parts):
        ffn(k)


def _out_ffn(x, ya, yb, yc, mod, norm, wo, wgu, wd, fmod, fnorm, l, tt, parts, final):
    B, S, D = x.shape
    tok = lambda width: pl.BlockSpec((B, tt, width), lambda i: (0, i, 0))
    return pl.pallas_call(
        functools.partial(_out_ffn_kernel, final=final, parts=parts),
        grid=(S // tt,),
        in_specs=[tok(D), tok(MLA_WIDTH), tok(RET_WIDTH),
                  pl.BlockSpec((tt, B, LRU_WIDTH), lambda i: (i, 0, 0)),
                  pl.BlockSpec((None, B, N_MOD, D), lambda i: (l, 0, 0, 0)),
                  pl.BlockSpec((None, 1, D), lambda i: (l, 0, 0)),
                  _resident((None, D, D), lambda i: (l, 0, 0)),
                  _resident((None, D, FFN_HIDDEN), lambda i: (l, 0, 0)),
                  _resident((None, D, FFN_HIDDEN), lambda i: (l, 0, 1)),
                  _resident((None, FFN_HIDDEN, D), lambda i: (l, 0, 0)),
                  pl.BlockSpec((B, 2, D), lambda i: (0, 0, 0)),
                  pl.BlockSpec((1, D), lambda i: (0, 0))],
        out_specs=tok(D),
        out_shape=jax.ShapeDtypeStruct((B, S, D), F32),
        scratch_shapes=[pltpu.VMEM((B * tt, D), BF16), pltpu.VMEM((B * tt, D), F32)],
        compiler_params=_params("parallel"),
        name="out_ffn",
    )(x, ya, yb, yc, mod, norm, wo, wgu, wgu, wd, fmod, fnorm)


def _split_in_proj(w_in):
    n_mla = MLA_Q_RANK + MLA_KV_RANK + MLA_ROPE
    w_mla = jnp.pad(w_in[..., :n_mla].astype(BF16), ((0, 0), (0, 0), (0, Z_MLA - n_mla)))
    return (w_mla, w_in[..., n_mla:n_mla + Z_RET].astype(BF16),
            w_in[..., n_mla + Z_RET:].astype(BF16))


def _pad_heads(w, width):
    L, R, H, _ = w.shape
    return jnp.pad(w, ((0, 0), (0, 0), (0, 0), (0, HEAD_PAD - width))).reshape(L, R, H * HEAD_PAD)


def _mla_weights(w_uq, w_ukv):
    L = w_uq.shape[0]
    wq = _pad_heads(w_uq.reshape(L, MLA_Q_RANK, MLA_HEADS, MLA_NOPE + MLA_ROPE), MLA_NOPE + MLA_ROPE)
    kv = w_ukv.reshape(L, MLA_KV_RANK, MLA_HEADS, MLA_NOPE + MLA_V)
    wk = _pad_heads(kv[..., :MLA_NOPE], MLA_NOPE)
    v = kv[..., MLA_NOPE:]
    odd = (jnp.arange(MLA_HEADS) % 2 == 1)[None, None, :, None]
    zero = jnp.zeros_like(v)
    wv = jnp.concatenate([jnp.where(odd, zero, v), jnp.where(odd, v, zero)], axis=-1)
    wv = wv.reshape(L, MLA_KV_RANK, MLA_PAD)
    r = jnp.arange(LANES)[:, None]
    c = jnp.arange(MLA_PAD)[None, :]
    wr = ((r < MLA_ROPE) & (c % HEAD_PAD == MLA_NOPE + r)).astype(BF16)
    ones = ((c % HEAD_PAD) == jnp.where((c // HEAD_PAD) % 2 == 0, MLA_V, 0)).astype(F32)
    wqt = jnp.swapaxes(wq, 1, 2).astype(BF16)
    wvt = jnp.swapaxes(wv, 1, 2).astype(BF16)
    return wqt, wk.astype(BF16), wr, wvt, ones.reshape(MLA_PAD, 1)


def _block_diag(w):
    L, G, I, J = w.shape
    eye = jnp.eye(G, dtype=w.dtype)
    return jnp.einsum('lgij,gh->lgihj', w, eye).reshape(L, G * I, G * J)


def kernel(x, c, positions, mod_w, mod_b, norm1, w_in, mla_q_norm, mla_w_uq, mla_kv_norm, mla_w_ukv, ret_gn, lru_conv_w, lru_conv_b, lru_w_a, lru_b_a, lru_w_i, lru_b_i, lru_lambda, w_out, norm2, w_gate_up, w_down, final_norm, final_mod_w, final_mod_b):
    B, S, D = x.shape
    L = mod_w.shape[0]
    tm = min(S, UNIT_ROWS)
    tt = UNITS_PER_STEP * tm // B
    tp = min(S, PREP_TOKENS)
    tq = min(S, ATTN_BLOCK)
    row = lambda a: a.reshape(L, 1, a.shape[-1])

    mod = _modulation(c, mod_w, mod_b, MOD_COLS).reshape(L, B, N_MOD, D)
    fmod = _modulation(c, final_mod_w[None], final_mod_b[None], D).reshape(B, 2, D)
    fnorm = final_norm.reshape(1, D)
    tables = _rope_tables(positions)

    w_in_groups = _split_in_proj(w_in)
    wqt, wk, wr, wvt, ones = _mla_weights(mla_w_uq, mla_w_ukv)
    wa = _block_diag(lru_w_a).astype(BF16)
    wi = _block_diag(lru_w_i).astype(BF16)
    wo = w_out.astype(BF16)
    wgu = w_gate_up.astype(BF16)
    wd = w_down.astype(BF16)
    ret_consts = _retention_consts()

    for l in range(L):
        zm, zr, yc = _in_proj_lru(x, mod, row(norm1), w_in_groups, lru_conv_w, row(lru_conv_b), wa,
                                  row(lru_b_a), wi, row(lru_b_i), row(lru_lambda), l, tt,
                                  UNITS_PER_STEP)
        qt, k, vt = _mla_prep(zm, tables, mla_q_norm, mla_kv_norm, wqt, wk, wr, wvt, ones,
                              l, tp, tq)
        ya, yb = _mixers(qt, k, vt, zr, tables[0], tables[1], ret_consts, row(ret_gn), l, tq)
        x = _out_ffn(x, ya, yb, yc, mod, row(norm2), wo, wgu, wd, fmod, fnorm, l, tt,
                     UNITS_PER_STEP, final=(l == L - 1))
    return x
```

```python
import functools

import jax
import jax.numpy as jnp
from jax import lax
from jax.experimental import pallas as pl
from jax.experimental.pallas import tpu as pltpu

D_MODEL = 1024
MLA_HEADS = 6
MLA_Q_RANK = 256
MLA_KV_RANK = 128
MLA_NOPE = 64
MLA_ROPE = 32
MLA_V = 64
MLA_WIDTH = MLA_HEADS * MLA_V
RET_HEADS = 4
RET_DK = 32
RET_DV = 64
RET_WIDTH = RET_HEADS * RET_DV
RET_CHUNK = 128
LRU_WIDTH = D_MODEL - MLA_WIDTH - RET_WIDTH
LRU_BLOCKS = 6
LRU_BLOCK = LRU_WIDTH // LRU_BLOCKS
CONV_WIDTH = 4
LRU_C = 8.0
FFN_HIDDEN = 2816
ROPE_BASE = 10000.0
NORM_EPS = 1e-6
LOG2_E = 1.4426950408889634
N_MOD = 6

LANES = 128
HEAD_PAD = LANES
MLA_PAD = MLA_HEADS * HEAD_PAD
Z_MLA = 512
Z_RET = 2 * RET_HEADS * RET_DK + 2 * RET_WIDTH
Z_LRU = 2 * LRU_WIDTH
FFN_CHUNK = 256
UNIT_ROWS = 256
IN_PROJ_UNITS = 8
FFN_UNITS = 4
PREP_TOKENS = 2048
ATTN_BLOCK = 256
MOD_COLS = 1536
VMEM_LIMIT = 56 * 1024 * 1024

BF16 = jnp.bfloat16
F32 = jnp.float32


def _params(*sem):
    return pltpu.CompilerParams(dimension_semantics=sem, vmem_limit_bytes=VMEM_LIMIT)


def _resident(shape, index_map):
    return pl.BlockSpec(shape, index_map, pipeline_mode=pl.Buffered(1))


def _rms(x, g):
    return x * lax.rsqrt(jnp.mean(x * x, axis=-1, keepdims=True) + NORM_EPS) * g


def _swap_halves(x, first_half):
    return jnp.where(first_half, pltpu.roll(x, LANES - 16, 1), pltpu.roll(x, 16, 1))


def _mod_kernel(c_ref, w_ref, b_ref, o_ref):
    c = c_ref[...]
    ca = (c * jax.nn.sigmoid(c)).astype(BF16)
    o_ref[...] = jnp.dot(ca, w_ref[...].astype(BF16), preferred_element_type=F32) + b_ref[...]


def _modulation(c, w, b, tn):
    L, D, N = w.shape
    B = c.shape[0]
    return pl.pallas_call(
        _mod_kernel,
        grid=(L, N // tn),
        in_specs=[pl.BlockSpec((B, D), lambda l, j: (0, 0)),
                  pl.BlockSpec((None, D, tn), lambda l, j: (l, 0, j)),
                  pl.BlockSpec((None, 1, tn), lambda l, j: (l, 0, j))],
        out_specs=pl.BlockSpec((None, B, tn), lambda l, j: (l, 0, j)),
        out_shape=jax.ShapeDtypeStruct((L, B, N), F32),
        compiler_params=_params("parallel", "parallel"),
        name="modulation",
    )(c, w, b.reshape(L, 1, N))


def _rope_table_kernel(pos_ref, inv_ref, cos_ref, sin_ref, cost_ref, sint_ref):
    ang = inv_ref[...] * pos_ref[...].astype(F32)
    cos = jnp.cos(ang)
    sin = jnp.sin(ang)
    cost_ref[...] = cos
    sint_ref[...] = sin
    reps = LANES // (2 * cos.shape[0])
    cos_ref[...] = jnp.concatenate([cos] * (2 * reps), axis=0).T
    sin_ref[...] = jnp.concatenate([-sin, sin] * reps, axis=0).T


def _rope_tables(positions):
    B, S = positions.shape
    half = MLA_ROPE // 2
    inv = ROPE_BASE ** (-jnp.arange(half, dtype=F32) / half)
    ts = min(S, 1024)
    spec = pl.BlockSpec((None, ts, LANES), lambda b, i: (b, i, 0))
    spec_t = pl.BlockSpec((None, half, ts), lambda b, i: (b, 0, i))
    return pl.pallas_call(
        _rope_table_kernel,
        grid=(B, S // ts),
        in_specs=[pl.BlockSpec((None, 1, ts), lambda b, i: (b, 0, i)),
                  pl.BlockSpec((half, 1), lambda b, i: (0, 0))],
        out_specs=[spec, spec, spec_t, spec_t],
        out_shape=[jax.ShapeDtypeStruct((B, S, LANES), F32)] * 2
        + [jax.ShapeDtypeStruct((B, half, S), F32)] * 2,
        compiler_params=_params("parallel", "parallel"),
        name="rope_tables",
    )(positions.reshape(B, 1, S), inv.reshape(half, 1))


def _softplus(x):
    return jnp.maximum(x, 0.0) + jnp.log1p(jnp.exp(-jnp.abs(x)))


def _gelu_tanh(x):
    return 0.5 * x * (1.0 + jnp.tanh(0.7978845608028654 * (x + 0.044715 * (x * x * x))))


def _sigmoid(x):
    return 0.5 * jnp.tanh(0.5 * x) + 0.5


def _in_proj_lru_kernel(x_ref, mod_ref, g_ref, wm_ref, wr_ref, wl_ref, cw_ref, cb_ref, wa_ref,
                        ba_ref, wi_ref, bi_ref, lam_ref, zm_ref, zr_ref, yc_ref, xbuf, h_s,
                        *, parts):
    B, tt, D = x_ref.shape
    th = tt // parts
    W = LRU_WIDTH
    PAD = 8
    taps = CONV_WIDTH - 1

    @pl.when(pl.program_id(0) == 0)
    def _():
        xbuf[0:PAD] = jnp.zeros((PAD, B, W), F32)
        h_s[...] = jnp.zeros((B, W), F32)

    mod = mod_ref[...]
    cw = cw_ref[...]
    log_a_rate = -LRU_C * _softplus(-lam_ref[...])

    def part(k):
        tok = slice(k * th, (k + 1) * th)
        h = _rms(x_ref[:, tok, :], g_ref[...]) * (1.0 + mod[:, 1:2]) + mod[:, 0:1]
        h = h.reshape(B * th, D).astype(BF16)

        zl = jnp.dot(h, wl_ref[...], preferred_element_type=F32)
        zm_ref[:, tok, :] = jnp.dot(h, wm_ref[...],
                                    preferred_element_type=F32).reshape(B, th, Z_MLA)
        zl = pltpu.einshape("btd->tbd", zl.reshape(B, th, Z_LRU))
        gate = _gelu_tanh(zl[:, :, W:])

        xbuf[PAD:PAD + th] = zl[:, :, :W]
        xc = cb_ref[...].reshape(1, 1, W) + sum(
            xbuf[PAD - taps + j:PAD - taps + j + th] * cw[j].reshape(1, 1, W)
            for j in range(CONV_WIDTH))
        xbuf[PAD - taps:PAD] = xbuf[PAD + th - taps:PAD + th]

        xc2 = xc.reshape(th * B, W)
        xb = xc2.astype(BF16)
        r = _sigmoid(jnp.dot(xb, wa_ref[...], preferred_element_type=F32) + ba_ref[...])
        i = _sigmoid(jnp.dot(xb, wi_ref[...], preferred_element_type=F32) + bi_ref[...])
        zr_ref[:, tok, :] = jnp.dot(h, wr_ref[...],
                                    preferred_element_type=F32).reshape(B, th, Z_RET)

        log_a = log_a_rate * r
        a = jnp.exp(log_a)
        one_minus_a2 = -jnp.tanh(log_a) * (a * a + 1.0)
        root = jnp.where(one_minus_a2 > 0.0, one_minus_a2 * lax.rsqrt(one_minus_a2), 0.0)
        a = a.reshape(th, B, W)
        b = (root * i * xc2).reshape(th, B, W)

        hid = h_s[...]
        for t in range(th):
            hid = a[t] * hid + b[t]
            yc_ref[k * th + t] = gate[t] * hid
        h_s[...] = hid

    for k in range(parts):
        part(k)


def _in_proj_lru(x, mod, norm, w_groups, cw, cb, wa, ba, wi, bi, lam, l, tt, parts):
    B, S, D = x.shape
    W = LRU_WIDTH
    tok = lambda width: pl.BlockSpec((B, tt, width), lambda i: (0, i, 0))
    vec = pl.BlockSpec((None, 1, W), lambda i: (l, 0, 0))
    mat = _resident((None, W, W), lambda i: (l, 0, 0))
    return pl.pallas_call(
        functools.partial(_in_proj_lru_kernel, parts=parts),
        grid=(S // tt,),
        in_specs=[tok(D),
                  pl.BlockSpec((None, B, N_MOD, D), lambda i: (l, 0, 0, 0)),
                  pl.BlockSpec((None, 1, D), lambda i: (l, 0, 0)),
                  *[_resident((None, D, w.shape[-1]), lambda i: (l, 0, 0)) for w in w_groups],
                  pl.BlockSpec((None, CONV_WIDTH, W), lambda i: (l, 0, 0)),
                  vec, mat, vec, mat, vec, vec],
        out_specs=[tok(Z_MLA), tok(Z_RET),
                   pl.BlockSpec((tt, B, W), lambda i: (i, 0, 0))],
        out_shape=[jax.ShapeDtypeStruct((B, S, Z_MLA), F32),
                   jax.ShapeDtypeStruct((B, S, Z_RET), F32),
                   jax.ShapeDtypeStruct((S, B, W), F32)],
        scratch_shapes=[pltpu.VMEM((tt // parts + 8, B, W), F32), pltpu.VMEM((B, W), F32)],
        compiler_params=_params("arbitrary"),
        name="in_proj_lru",
    )(x, mod, norm, *w_groups, cw, cb, wa, ba, wi, bi, lam)


def _rms_feature_major(x, g):
    return x * lax.rsqrt(jnp.mean(x * x, axis=0, keepdims=True) + NORM_EPS) * g


def _mla_prep_kernel(z_ref, cos_ref, sin_ref, cost_ref, sint_ref, qn_ref, kvn_ref, kvnr_ref,
                     wqt_ref, wk_ref, wr_ref, wvt_ref, one_ref, qt_ref, k_ref, vt_ref, *, tq):
    z = z_ref[...]
    cos = cos_ref[...]
    sin = sin_ref[...]
    lane = lax.broadcasted_iota(jnp.int32, cos.shape, 1)
    first_half = (lane % 32) < 16

    c_kv = _rms(z[:, MLA_Q_RANK:MLA_Q_RANK + MLA_KV_RANK], kvnr_ref[...]).astype(BF16)
    kr = z[:, MLA_Q_RANK + MLA_KV_RANK:]
    kr = jnp.where(lane < MLA_ROPE, kr * cos + _swap_halves(kr, first_half) * sin, 0.0)
    k = (jnp.dot(c_kv, wk_ref[...], preferred_element_type=F32)
         + jnp.dot(kr.astype(BF16), wr_ref[...], preferred_element_type=F32))
    k_ref[...] = k.astype(BF16)

    zt = z[:, :MLA_Q_RANK + MLA_KV_RANK].T
    c_q = _rms_feature_major(zt[:MLA_Q_RANK], qn_ref[...]).astype(BF16)
    q = jnp.dot(wqt_ref[...], c_q, preferred_element_type=F32)
    qk_scale = (MLA_NOPE + MLA_ROPE) ** -0.5 * LOG2_E
    cos_t = cost_ref[...] * qk_scale
    sin_t = sint_ref[...] * qk_scale
    half = MLA_ROPE // 2
    rows = []
    for h in range(MLA_HEADS):
        r0 = h * HEAD_PAD + MLA_NOPE
        x1 = q[r0:r0 + half]
        x2 = q[r0 + half:r0 + MLA_ROPE]
        rows += [q[h * HEAD_PAD:r0] * qk_scale, x1 * cos_t - x2 * sin_t, x2 * cos_t + x1 * sin_t,
                 q[r0 + MLA_ROPE:(h + 1) * HEAD_PAD]]
    q = jnp.concatenate(rows, axis=0).astype(BF16)
    c_kv_t = _rms_feature_major(zt[MLA_Q_RANK:], kvn_ref[...]).astype(BF16)
    v = (jnp.dot(wvt_ref[...], c_kv_t, preferred_element_type=F32) + one_ref[...]).astype(BF16)
    for t in range(z.shape[0] // tq):
        qt_ref[t] = q[:, t * tq:(t + 1) * tq]
        vt_ref[t] = v[:, t * tq:(t + 1) * tq]


def _mla_prep(zm, tables, qn, kvn, wqt, wk, wr, wvt, ones, l, tm, tq):
    B, S, _ = zm.shape
    cos, sin, cos_t, sin_t = tables
    tok = lambda w: pl.BlockSpec((None, tm, w), lambda b, i: (b, i, 0))
    tab_t = pl.BlockSpec((None, MLA_ROPE // 2, tm), lambda b, i: (b, 0, i))
    col = lambda n: pl.BlockSpec((None, n, 1), lambda b, i: (l, 0, 0))
    tr_spec = pl.BlockSpec((None, tm // tq, MLA_PAD, tq), lambda b, i: (b, i, 0, 0))
    tr_shape = jax.ShapeDtypeStruct((B, S // tq, MLA_PAD, tq), BF16)
    L = qn.shape[0]
    return pl.pallas_call(
        functools.partial(_mla_prep_kernel, tq=tq),
        grid=(B, S // tm),
        in_specs=[tok(Z_MLA), tok(LANES), tok(LANES), tab_t, tab_t,
                  col(MLA_Q_RANK), col(MLA_KV_RANK),
                  pl.BlockSpec((None, 1, MLA_KV_RANK), lambda b, i: (l, 0, 0)),
                  _resident((None, MLA_PAD, MLA_Q_RANK), lambda b, i: (l, 0, 0)),
                  _resident((None, MLA_KV_RANK, MLA_PAD), lambda b, i: (l, 0, 0)),
                  _resident((LANES, MLA_PAD), lambda b, i: (0, 0)),
                  _resident((None, MLA_PAD, MLA_KV_RANK), lambda b, i: (l, 0, 0)),
                  pl.BlockSpec((MLA_PAD, 1), lambda b, i: (0, 0))],
        out_specs=[tr_spec, tok(MLA_PAD), tr_spec],
        out_shape=[tr_shape, jax.ShapeDtypeStruct((B, S, MLA_PAD), BF16), tr_shape],
        compiler_params=_params("parallel", "parallel"),
        name="mla_prep",
    )(zm, cos, sin, cos_t, sin_t, qn.reshape(L, -1, 1), kvn.reshape(L, -1, 1),
      kvn.reshape(L, 1, -1), wqt, wk, wr, wvt, ones)


def _retention_consts():
    C, H = RET_CHUNK, RET_HEADS
    log_g = jnp.log(1.0 - jnp.exp2(-5.0 - jnp.arange(H, dtype=F32)))
    idx = jnp.arange(C, dtype=F32)
    diff = idx[:, None] - idx[None, :]
    decay = jnp.where(diff >= 0, jnp.exp(log_g[:, None, None] * jnp.maximum(diff, 0.0)), 0.0)
    q_decay = jnp.exp(log_g[:, None] * (idx + 1.0))
    k_decay = jnp.exp(log_g[:, None] * (C - 1.0 - idx))
    chunk_decay = jnp.exp(log_g * C)
    qd = jnp.repeat(q_decay.T, RET_DK, axis=1)
    kd = jnp.repeat(k_decay.T, RET_DK, axis=1)
    cd = jnp.broadcast_to(jnp.repeat(chunk_decay, RET_DK)[:, None], (H * RET_DK, RET_WIDTH))
    return decay, qd, kd, cd


def _mixers_kernel(qt_ref, k_ref, vt_ref, z_ref, cos_ref, sin_ref, dec_ref, qd_ref, kd_ref,
                   cd_ref, gn_ref, ya_ref, yb_ref, s_s, state_s, acc_s, o_s, rq_s, rk_s,
                   *, tq, nq):
    groups = tq // 8
    key = lax.broadcasted_iota(jnp.int32, (tq, tq), 0)
    qry = lax.broadcasted_iota(jnp.int32, (tq, tq), 1)
    causal = key <= qry
    value_rows = lax.broadcasted_iota(jnp.int32, (HEAD_PAD, tq), 0) < MLA_V
    hs = lambda h: slice(h * HEAD_PAD, (h + 1) * HEAD_PAD)

    C = RET_CHUNK
    per_block = tq // C
    lane = lax.broadcasted_iota(jnp.int32, (C, LANES), 1)
    first_half = (lane % 32) < 16
    low = lane < RET_DV
    head_of_lane = lane // RET_DK
    srow = lax.broadcasted_iota(jnp.int32, (LANES, RET_WIDTH), 0) // RET_DK
    scol = lax.broadcasted_iota(jnp.int32, (LANES, RET_WIDTH), 1) // RET_DV
    same_head = srow == scol
    gn = gn_ref[...]
    cd = cd_ref[...]

    def chunk_rows(ib):
        return [slice((ib * per_block + c) * C, (ib * per_block + c + 1) * C)
                for c in range(per_block)]

    def retention_prepare(ib):
        for c, r in enumerate(chunk_rows(ib)):
            cos = cos_ref[r, :]
            sin = sin_ref[r, :]
            xq = z_ref[r, 0:LANES]
            xk = z_ref[r, LANES:2 * LANES]
            rq_s[c] = xq * cos + _swap_halves(xq, first_half) * sin
            rk_s[c] = (xk * cos + _swap_halves(xk, first_half) * sin) * (RET_DK ** -0.5)

    def attention_finish(ib):
        for p in range(MLA_HEADS // 2):
            even = acc_s[2 * p]
            odd = acc_s[2 * p + 1]
            pair = jnp.where(value_rows, even / even[MLA_V:MLA_V + 1], odd / odd[0:1])
            ya_ref[ib * tq:(ib + 1) * tq, p * LANES:(p + 1) * LANES] = pair.T.astype(BF16)

    def retention_finish(ib):
        for c, r in enumerate(chunk_rows(ib)):
            for p in range(RET_HEADS // 2):
                cols = slice(p * LANES, (p + 1) * LANES)
                o = o_s[c, :, cols]
                s_lo = jnp.sum(jnp.where(low, o, 0.0), axis=-1, keepdims=True)
                s_hi = jnp.sum(jnp.where(low, 0.0, o), axis=-1, keepdims=True)
                d = o - jnp.where(low, s_lo, s_hi) * (1.0 / RET_DV)
                d2 = d * d
                v_lo = jnp.sum(jnp.where(low, d2, 0.0), axis=-1, keepdims=True)
                v_hi = jnp.sum(jnp.where(low, 0.0, d2), axis=-1, keepdims=True)
                var = jnp.where(low, v_lo, v_hi) * (1.0 / RET_DV)
                y = d * lax.rsqrt(var + NORM_EPS) * gn[:, cols]
                gp = z_ref[r, 2 * LANES + RET_WIDTH + p * LANES:
                           2 * LANES + RET_WIDTH + (p + 1) * LANES]
                yb_ref[r, cols] = (gp * jax.nn.sigmoid(gp) * y).astype(BF16)

    def block(i):
        if i == 0:
            retention_prepare(0)
        rows = chunk_rows(i)
        q = [rq_s[c] for c in range(per_block)]
        k = [rk_s[c] for c in range(per_block)]
        kb = [x.astype(BF16) for x in k]
        v = [z_ref[r, 2 * LANES:2 * LANES + RET_WIDTH].astype(BF16) for r in rows]
        if i > 0:
            attention_finish(i - 1)
            retention_finish(i - 1)

        def scores(h):
            m = None
            for j in range(i + 1):
                s = jnp.dot(k_ref[j * tq:(j + 1) * tq, hs(h)], qt_ref[i, hs(h), :],
                            preferred_element_type=F32)
                if j == i:
                    s = jnp.where(causal, s, -jnp.inf)
                s_s[h, j] = s
                mj = jnp.max(s.reshape(groups, 8, tq), axis=0)
                m = mj if m is None else jnp.maximum(m, mj)
            return jnp.broadcast_to(jnp.max(m, axis=0, keepdims=True), (8, tq))

        def values(h, m):
            acc = None
            for j in range(i + 1):
                p = jnp.exp2(s_s[h, j].reshape(groups, 8, tq) - m[None])
                pv = jnp.dot(vt_ref[j, hs(h), :], p.reshape(tq, tq).astype(BF16),
                             preferred_element_type=F32)
                acc = pv if acc is None else acc + pv
            return acc

        m_next = scores(0)
        sr = [[(lax.dot_general(jnp.where(head_of_lane == h, q[c], 0.0).astype(BF16), kb[c],
                                (((1,), (1,)), ((), ())), preferred_element_type=F32)
                * dec_ref[h]).astype(BF16) for h in range(RET_HEADS)] for c in range(per_block)]
        kv = [lax.dot_general((k[c] * kd_ref[...]).astype(BF16), v[c], (((0,), (0,)), ((), ())),
                              preferred_element_type=F32) for c in range(per_block)]
        for h in range(MLA_HEADS):
            m = m_next
            if h + 1 < MLA_HEADS:
                m_next = scores(h + 1)
            acc_s[h] = values(h, m)
            if h == MLA_HEADS // 2 - 1:
                states = [jnp.zeros((LANES, RET_WIDTH), F32) if i == 0 else state_s[...]]
                for c in range(per_block):
                    states.append(states[c] * cd + jnp.where(same_head, kv[c], 0.0))
                state_s[...] = states[per_block]
                for c in range(per_block):
                    cross = jnp.dot((q[c] * qd_ref[...]).astype(BF16), states[c].astype(BF16),
                                    preferred_element_type=F32)
                    for p in range(RET_HEADS // 2):
                        cols = slice(p * LANES, (p + 1) * LANES)
                        inner = jnp.where(
                            low,
                            jnp.dot(sr[c][2 * p], v[c][:, cols], preferred_element_type=F32),
                            jnp.dot(sr[c][2 * p + 1], v[c][:, cols],
                                    preferred_element_type=F32))
                        o_s[c, :, cols] = inner + cross[:, cols]
        if i + 1 < nq:
            retention_prepare(i + 1)

    def body(i, carry):
        lax.switch(i, [functools.partial(block, n) for n in range(nq)])
        return carry

    lax.fori_loop(0, nq, body, 0)
    attention_finish(nq - 1)
    retention_finish(nq - 1)


def _mixers(qt, k, vt, zr, cos, sin, consts, gn, l, tq):
    B, S, _ = k.shape
    nq = S // tq
    decay, qd, kd, cd = consts
    tr_spec = pl.BlockSpec((None, nq, MLA_PAD, tq), lambda b: (b, 0, 0, 0))
    row = lambda width: pl.BlockSpec((None, S, width), lambda b: (b, 0, 0))
    full = lambda a: pl.BlockSpec(a.shape, lambda b: (0,) * a.ndim)
    return pl.pallas_call(
        functools.partial(_mixers_kernel, tq=tq, nq=nq),
        grid=(B,),
        in_specs=[tr_spec, row(MLA_PAD), tr_spec, row(Z_RET), row(LANES), row(LANES),
                  full(decay), full(qd), full(kd), full(cd),
                  pl.BlockSpec((None, 1, RET_WIDTH), lambda b: (l, 0, 0))],
        out_specs=[row(MLA_WIDTH), row(RET_WIDTH)],
        out_shape=[jax.ShapeDtypeStruct((B, S, MLA_WIDTH), BF16),
                   jax.ShapeDtypeStruct((B, S, RET_WIDTH), BF16)],
        scratch_shapes=[pltpu.VMEM((MLA_HEADS, nq, tq, tq), F32),
                        pltpu.VMEM((LANES, RET_WIDTH), F32),
                        pltpu.VMEM((MLA_HEADS, HEAD_PAD, tq), F32),
                        pltpu.VMEM((tq // RET_CHUNK, RET_CHUNK, RET_WIDTH), F32),
                        pltpu.VMEM((tq // RET_CHUNK, RET_CHUNK, LANES), F32),
                        pltpu.VMEM((tq // RET_CHUNK, RET_CHUNK, LANES), F32)],
        compiler_params=_params("parallel"),
        name="mixers",
    )(qt, k, vt, zr, cos, sin, decay, qd, kd, cd, gn)


def _out_ffn_kernel(x_ref, ya_ref, yb_ref, yc_ref, mod_ref, g_ref, wo_ref, wg_ref, wu_ref,
                    wd_ref, fmod_ref, fg_ref, o_ref, h_s, acc_s, *, final, parts):
    B, tt, D = x_ref.shape
    th = tt // parts
    rows = B * th
    mod = mod_ref[...]
    a0, b0 = MLA_WIDTH, MLA_WIDTH + RET_WIDTH
    n_chunks = FFN_HIDDEN // FFN_CHUNK

    def out_proj(k):
        tok = slice(k * th, (k + 1) * th)
        r = slice(k * rows, (k + 1) * rows)
        yc = pltpu.einshape("tbd->btd", yc_ref[tok]).reshape(rows, LRU_WIDTH).astype(BF16)
        y = (jnp.dot(ya_ref[:, tok, :].reshape(rows, MLA_WIDTH), wo_ref[0:a0, :],
                     preferred_element_type=F32)
             + jnp.dot(yb_ref[:, tok, :].reshape(rows, RET_WIDTH), wo_ref[a0:b0, :],
                       preferred_element_type=F32)
             + jnp.dot(yc, wo_ref[b0:, :], preferred_element_type=F32))
        x1 = x_ref[:, tok, :] + mod[:, 2:3] * y.reshape(B, th, D)
        o_ref[:, tok, :] = x1
        h = _rms(x1, g_ref[...]) * (1.0 + mod[:, 4:5]) + mod[:, 3:4]
        h_s[r, :] = h.reshape(rows, D).astype(BF16)

    def ffn(k):
        tok = slice(k * th, (k + 1) * th)
        r = slice(k * rows, (k + 1) * rows)

        def gate_up(j):
            cols = slice(j * FFN_CHUNK, (j + 1) * FFN_CHUNK)
            h = h_s[r, :]
            return (jnp.dot(h, wg_ref[:, cols], preferred_element_type=F32),
                    jnp.dot(h, wu_ref[:, cols], preferred_element_type=F32))

        nxt = gate_up(0)
        for j in range(n_chunks):
            gate, up = nxt
            if j + 1 < n_chunks:
                nxt = gate_up(j + 1)
            act = (gate * jax.nn.sigmoid(gate) * up).astype(BF16)
            down = jnp.dot(act, wd_ref[j * FFN_CHUNK:(j + 1) * FFN_CHUNK, :],
                           preferred_element_type=F32)
            acc_s[r, :] = down if j == 0 else acc_s[r, :] + down
        x2 = o_ref[:, tok, :] + mod[:, 5:6] * acc_s[r, :].reshape(B, th, D)
        if final:
            fmod = fmod_ref[...]
            x2 = _rms(x2, fg_ref[...]) * (1.0 + fmod[:, 1:2]) + fmod[:, 0:1]
        o_ref[:, tok, :] = x2

    for k in range(parts):
        out_proj(k)
    for k in range(parts):
        ffn(k)


def _out_ffn(x, ya, yb, yc, mod, norm, wo, wgu, wd, fmod, fnorm, l, tt, parts, final):
    B, S, D = x.shape
    tok = lambda width: pl.BlockSpec((B, tt, width), lambda i: (0, i, 0))
    return pl.pallas_call(
        functools.partial(_out_ffn_kernel, final=final, parts=parts),
        grid=(S // tt,),
        in_specs=[tok(D), tok(MLA_WIDTH), tok(RET_WIDTH),
                  pl.BlockSpec((tt, B, LRU_WIDTH), lambda i: (i, 0, 0)),
                  pl.BlockSpec((None, B, N_MOD, D), lambda i: (l, 0, 0, 0)),
                  pl.BlockSpec((None, 1, D), lambda i: (l, 0, 0)),
                  _resident((None, D, D), lambda i: (l, 0, 0)),
                  _resident((None, D, FFN_HIDDEN), lambda i: (l, 0, 0)),
                  _resident((None, D, FFN_HIDDEN), lambda i: (l, 0, 1)),
                  _resident((None, FFN_HIDDEN, D), lambda i: (l, 0, 0)),
                  pl.BlockSpec((B, 2, D), lambda i: (0, 0, 0)),
                  pl.BlockSpec((1, D), lambda i: (0, 0))],
        out_specs=tok(D),
        out_shape=jax.ShapeDtypeStruct((B, S, D), F32),
        scratch_shapes=[pltpu.VMEM((B * tt, D), BF16), pltpu.VMEM((B * tt, D), F32)],
        compiler_params=_params("parallel"),
        name="out_ffn",
    )(x, ya, yb, yc, mod, norm, wo, wgu, wgu, wd, fmod, fnorm)


def _split_in_proj(w_in):
    n_mla = MLA_Q_RANK + MLA_KV_RANK + MLA_ROPE
    w_mla = jnp.pad(w_in[..., :n_mla].astype(BF16), ((0, 0), (0, 0), (0, Z_MLA - n_mla)))
    return (w_mla, w_in[..., n_mla:n_mla + Z_RET].astype(BF16),
            w_in[..., n_mla + Z_RET:].astype(BF16))


def _pad_heads(w, width):
    L, R, H, _ = w.shape
    return jnp.pad(w, ((0, 0), (0, 0), (0, 0), (0, HEAD_PAD - width))).reshape(L, R, H * HEAD_PAD)


def _mla_weights(w_uq, w_ukv):
    L = w_uq.shape[0]
    wq = _pad_heads(w_uq.reshape(L, MLA_Q_RANK, MLA_HEADS, MLA_NOPE + MLA_ROPE), MLA_NOPE + MLA_ROPE)
    kv = w_ukv.reshape(L, MLA_KV_RANK, MLA_HEADS, MLA_NOPE + MLA_V)
    wk = _pad_heads(kv[..., :MLA_NOPE], MLA_NOPE)
    v = kv[..., MLA_NOPE:]
    odd = (jnp.arange(MLA_HEADS) % 2 == 1)[None, None, :, None]
    zero = jnp.zeros_like(v)
    wv = jnp.concatenate([jnp.where(odd, zero, v), jnp.where(odd, v, zero)], axis=-1)
    wv = wv.reshape(L, MLA_KV_RANK, MLA_PAD)
    r = jnp.arange(LANES)[:, None]
    c = jnp.arange(MLA_PAD)[None, :]
    wr = ((r < MLA_ROPE) & (c % HEAD_PAD == MLA_NOPE + r)).astype(BF16)
    ones = ((c % HEAD_PAD) == jnp.where((c // HEAD_PAD) % 2 == 0, MLA_V, 0)).astype(F32)
    wqt = jnp.swapaxes(wq, 1, 2).astype(BF16)
    wvt = jnp.swapaxes(wv, 1, 2).astype(BF16)
    return wqt, wk.astype(BF16), wr, wvt, ones.reshape(MLA_PAD, 1)


def _block_diag(w):
    L, G, I, J = w.shape
    eye = jnp.eye(G, dtype=w.dtype)
    return jnp.einsum('lgij,gh->lgihj', w, eye).reshape(L, G * I, G * J)


def kernel(x, c, positions, mod_w, mod_b, norm1, w_in, mla_q_norm, mla_w_uq, mla_kv_norm, mla_w_ukv, ret_gn, lru_conv_w, lru_conv_b, lru_w_a, lru_b_a, lru_w_i, lru_b_i, lru_lambda, w_out, norm2, w_gate_up, w_down, final_norm, final_mod_w, final_mod_b):
    B, S, D = x.shape
    L = mod_w.shape[0]
    tm = min(S, UNIT_ROWS)
    tt_in = IN_PROJ_UNITS * tm // B
    tt_ffn = FFN_UNITS * tm // B
    tp = min(S, PREP_TOKENS)
    tq = min(S, ATTN_BLOCK)
    row = lambda a: a.reshape(L, 1, a.shape[-1])

    mod = _modulation(c, mod_w, mod_b, MOD_COLS).reshape(L, B, N_MOD, D)
    fmod = _modulation(c, final_mod_w[None], final_mod_b[None], D).reshape(B, 2, D)
    fnorm = final_norm.reshape(1, D)
    tables = _rope_tables(positions)

    w_in_groups = _split_in_proj(w_in)
    wqt, wk, wr, wvt, ones = _mla_weights(mla_w_uq, mla_w_ukv)
    wa = _block_diag(lru_w_a).astype(BF16)
    wi = _block_diag(lru_w_i).astype(BF16)
    wo = w_out.astype(BF16)
    wgu = w_gate_up.astype(BF16)
    wd = w_down.astype(BF16)
    ret_consts = _retention_consts()

    for l in range(L):
        zm, zr, yc = _in_proj_lru(x, mod, row(norm1), w_in_groups, lru_conv_w, row(lru_conv_b), wa,
                                  row(lru_b_a), wi, row(lru_b_i), row(lru_lambda), l, tt_in,
                                  IN_PROJ_UNITS)
        qt, k, vt = _mla_prep(zm, tables, mla_q_norm, mla_kv_norm, wqt, wk, wr, wvt, ones,
                              l, tp, tq)
        ya, yb = _mixers(qt, k, vt, zr, tables[0], tables[1], ret_consts, row(ret_gn), l, tq)
        x = _out_ffn(x, ya, yb, yc, mod, row(norm2), wo, wgu, wd, fmod, fnorm, l, tt_ffn,
                     FFN_UNITS, final=(l == L - 1))
    return x
```
